```python
import jax, jax.numpy as jnp
from jax import lax
import numpy as np

D_MODEL = 1024
BATCH = 8
SEQ = 4096
DEPTH = 2

CTX_LEN = 256
GRID_W = 64
HEAD_DIM = 64
ROPE_BASE = 10000.0
NORM_EPS = 1e-6
NEG_INF = -1e30
NA_HEADS = 8
NA_WIN_ROWS = 8
NA_WIN_COLS = 16
SW_Q_HEADS = 8
SW_KV_HEADS = 2
SW_WINDOW = 128
SW_BLOCK = 128
MLA_HEADS = 8
MLA_Q_RANK = 384
MLA_KV_RANK = 256
MLA_NOPE = 64
MLA_ROPE = 32
MLA_V = 64
MLA_BLOCK = 128
N_BRANCH = 3
BRANCH_W = 512
D_FF = 2816
CONV_W = 3

IN_SIZES = (NA_HEADS * HEAD_DIM, NA_HEADS * HEAD_DIM, NA_HEADS * HEAD_DIM,
            SW_Q_HEADS * HEAD_DIM, SW_KV_HEADS * HEAD_DIM, SW_KV_HEADS * HEAD_DIM,
            MLA_Q_RANK, MLA_KV_RANK, MLA_ROPE, N_BRANCH * D_MODEL)
IN_SPLITS = tuple(sum(IN_SIZES[:i + 1]) for i in range(len(IN_SIZES) - 1))
D_IN = sum(IN_SIZES)

kernel_name = "hybrid_na_swa_mla_convffn_prefix_dit"

f32 = jnp.float32


def rms_norm(x, g):
    xf = x.astype(f32)
    y = xf * lax.rsqrt(jnp.mean(xf * xf, axis=-1, keepdims=True) + NORM_EPS)
    return (y * g.astype(f32)).astype(x.dtype)


def modulate(h, shift, scale):
    return h * (1 + scale) + shift


def joint_softmax(parts, dtype):
    sizes = [p.shape[-1] for p in parts]
    s = jnp.concatenate([p.astype(f32) for p in parts], axis=-1)
    p = jax.nn.softmax(s, axis=-1).astype(dtype)
    idx = [sum(sizes[:i + 1]) for i in range(len(sizes) - 1)]
    return jnp.split(p, idx, axis=-1)


def _rope_1d(xp, pos):
    n = xp.shape[-1] // 2
    inv = ROPE_BASE ** (-jnp.arange(n, dtype=f32) / n)
    ang = pos.astype(f32)[:, None] * inv[None, :]
    cos = jnp.cos(ang)[None, :, None, :].astype(xp.dtype)
    sin = jnp.sin(ang)[None, :, None, :].astype(xp.dtype)
    x1, x2 = xp[..., :n], xp[..., n:]
    return jnp.concatenate([x1 * cos - x2 * sin, x1 * sin + x2 * cos], axis=-1)


def axial_rope(x, row, col):
    half = x.shape[-1] // 2
    return jnp.concatenate([_rope_1d(x[..., :half], row), _rope_1d(x[..., half:], col)], axis=-1)


def project_streams(h, w_in, na_qn, na_kn, sw_qn, sw_kn, mla_qrn, mla_kvrn, w_uq, w_ukv, mla_qn, mla_kn):
    B, L, _ = h.shape
    (na_q, na_k, na_v, sw_q, sw_k, sw_v, c_q, c_kv, k_r, gates) = jnp.split(h @ w_in, IN_SPLITS, axis=-1)
    heads = lambda t, n: t.reshape(B, L, n, -1)
    na_q = rms_norm(heads(na_q, NA_HEADS), na_qn)
    na_k = rms_norm(heads(na_k, NA_HEADS), na_kn)
    na_v = heads(na_v, NA_HEADS)
    sw_q = rms_norm(heads(sw_q, SW_Q_HEADS), sw_qn)
    sw_k = rms_norm(heads(sw_k, SW_KV_HEADS), sw_kn)
    sw_v = heads(sw_v, SW_KV_HEADS)
    mq = heads(rms_norm(c_q, mla_qrn) @ w_uq, MLA_HEADS)
    kv = heads(rms_norm(c_kv, mla_kvrn) @ w_ukv, MLA_HEADS)
    k_rope = jnp.broadcast_to(k_r[:, :, None, :], (B, L, MLA_HEADS, MLA_ROPE))
    mk = jnp.concatenate([kv[..., :MLA_NOPE], k_rope], axis=-1)
    mq = rms_norm(mq, mla_qn)
    mk = rms_norm(mk, mla_kn)
    mv = kv[..., MLA_NOPE:]
    return (na_q, na_k, na_v, sw_q, sw_k, sw_v, mq, mk, mv, gates)


def ctx_attend(q, k, v, sink=None):
    B, L, Hq, d = q.shape
    Hkv = k.shape[2]
    G = Hq // Hkv
    qg = q.reshape(B, L, Hkv, G, d)
    s = jnp.einsum('blkgd,bmkd->bkglm', qg, k) * (d ** -0.5)
    parts = [s]
    if sink is not None:
        parts.append(jnp.broadcast_to(sink.reshape(1, Hkv, G, 1, 1), (B, Hkv, G, L, 1)))
    p = joint_softmax(parts, v.dtype)[0]
    o = jnp.einsum('bkglm,bmkd->blkgd', p, v)
    return o.reshape(B, L, Hq * v.shape[-1])


def na_latent(q, k, v, kc, vc, rpb):
    B, S, H, d = q.shape
    rows = S // GRID_W
    wr = min(NA_WIN_ROWS, rows)
    r = jnp.arange(rows)
    key_rows = jnp.clip(r - wr // 2, 0, rows - wr)[:, None] + jnp.arange(wr)[None, :]
    col = jnp.arange(GRID_W)
    c0 = jnp.clip(col - NA_WIN_COLS // 2, 0, GRID_W - NA_WIN_COLS)
    in_win = (col[None, :] >= c0[:, None]) & (col[None, :] < c0[:, None] + NA_WIN_COLS)
    dr = key_rows - r[:, None] + (NA_WIN_ROWS - 1)
    dc = jnp.clip(col[None, :] - col[:, None], -(NA_WIN_COLS - 1), NA_WIN_COLS - 1) + (NA_WIN_COLS - 1)
    bias = rpb[:, dr[:, None, :, None], dc[None, :, None, :]]
    qg = q.reshape(B, rows, GRID_W, H, d)
    kg = k.reshape(B, rows, GRID_W, H, d)[:, key_rows]
    vg = v.reshape(B, rows, GRID_W, H, d)[:, key_rows].reshape(B, rows, wr * GRID_W, H, d)
    scale = d ** -0.5
    s_loc = jnp.einsum('brqhd,brikhd->bhrqik', qg, kg).astype(f32) * scale + bias.astype(f32)[None]
    s_loc = jnp.where(in_win[:, None, :], s_loc, NEG_INF).reshape(B, H, rows, GRID_W, wr * GRID_W)
    s_ctx = jnp.einsum('brqhd,bchd->bhrqc', qg, kc) * scale
    p_loc, p_ctx = joint_softmax([s_loc, s_ctx], v.dtype)
    o = jnp.einsum('bhrqn,brnhd->brqhd', p_loc, vg) + jnp.einsum('bhrqc,bchd->brqhd', p_ctx, vc)
    return o.reshape(B, S, H * d)


def sw_latent(q, k, v, kc, vc, sink):
    B, S, Hq, d = q.shape
    Hkv = k.shape[2]
    G = Hq // Hkv
    nb = S // SW_BLOCK
    qb = q.reshape(B, nb, SW_BLOCK, Hkv, G, d)

    def band(t):
        tp = jnp.pad(t, ((0, 0), (SW_BLOCK, SW_BLOCK), (0, 0), (0, 0)))
        tp = tp.reshape(B, nb + 2, SW_BLOCK, Hkv, t.shape[-1])
        return jnp.concatenate([tp[:, :-2], tp[:, 1:-1], tp[:, 2:]], axis=2)

    kb, vb = band(k), band(v)
    i = jnp.arange(SW_BLOCK)[:, None]
    j = jnp.arange(3 * SW_BLOCK)[None, :]
    kpos = jnp.arange(nb)[:, None, None] * SW_BLOCK + j[None] - SW_BLOCK
    mask = (jnp.abs(j - SW_BLOCK - i)[None] <= SW_WINDOW) & (kpos >= 0) & (kpos < S)
    scale = d ** -0.5
    s_loc = jnp.einsum('bnqkgd,bnjkd->bkgnqj', qb, kb).astype(f32) * scale
    s_loc = jnp.where(mask, s_loc, NEG_INF)
    s_ctx = jnp.einsum('bnqkgd,bckd->bkgnqc', qb, kc) * scale
    s_sink = jnp.broadcast_to(sink.reshape(1, Hkv, G, 1, 1, 1), (B, Hkv, G, nb, SW_BLOCK, 1))
    p_loc, p_ctx, _ = joint_softmax([s_loc, s_ctx, s_sink], v.dtype)
    o = jnp.einsum('bkgnqj,bnjkd->bnqkgd', p_loc, vb) + jnp.einsum('bkgnqc,bckd->bnqkgd', p_ctx, vc)
    return o.reshape(B, S, Hq * d)


def mla_latent(q, k, v, kc, vc):
    B, S, H, dq = q.shape
    nb = S // MLA_BLOCK
    scale = dq ** -0.5
    qb = jnp.moveaxis(q.reshape(B, nb, MLA_BLOCK, H, dq), 1, 0)

    def one_block(qi):
        s_loc = jnp.einsum('bqhd,bkhd->bhqk', qi, k) * scale
        s_ctx = jnp.einsum('bqhd,bchd->bhqc', qi, kc) * scale
        p_loc, p_ctx = joint_softmax([s_loc, s_ctx], v.dtype)
        return jnp.einsum('bhqk,bkhd->bqhd', p_loc, v) + jnp.einsum('bhqc,bchd->bqhd', p_ctx, vc)

    o = lax.map(one_block, qb)
    return jnp.moveaxis(o, 0, 1).reshape(B, S, H * v.shape[-1])


def merge_branches(o_na, o_sw, o_mla, gates, w_branch, w_out):
    B, L, _ = gates.shape
    g = jax.nn.sigmoid(gates.astype(f32)).astype(gates.dtype).reshape(B, L, N_BRANCH, D_MODEL)
    o = jnp.stack([o_na, o_sw, o_mla], axis=2)
    y = jnp.einsum('blnc,ncd->blnd', o, w_branch)
    return jnp.einsum('blnd,blnd->bld', g, y) @ w_out


def conv_ffn(h, w_up, conv_w, conv_b, w_down):
    u = h @ w_up
    C = u.shape[-1]
    u = lax.conv_general_dilated(u, conv_w[:, None, :].astype(u.dtype), window_strides=(1,),
                                 padding=[(CONV_W // 2, CONV_W // 2)],
                                 dimension_numbers=('NWC', 'WIO', 'NWC'),
                                 feature_group_count=C) + conv_b
    g, val = jnp.split(u, 2, axis=-1)
    return (jax.nn.silu(g) * val) @ w_down


def trunk_layer(x, ctx, mod, mod_c, g_mix, g_ffn, w_in, na_qn, na_kn, rpb, sw_qn, sw_kn, sink,
                mla_qrn, mla_kvrn, w_uq, w_ukv, mla_qn, mla_kn, w_branch, w_out,
                w_up, conv_w, conv_b, w_down, need_ctx):
    B, S, _ = x.shape
    shift_a, scale_a, gate_a, shift_f, scale_f, gate_f = jnp.split(mod, 6, axis=-1)
    cshift_a, cscale_a, cgate_a, cshift_f, cscale_f, cgate_f = jnp.split(mod_c, 6, axis=-1)
    proj = lambda h: project_streams(h, w_in, na_qn, na_kn, sw_qn, sw_kn, mla_qrn, mla_kvrn,
                                     w_uq, w_ukv, mla_qn, mla_kn)
    hx = modulate(rms_norm(x, g_mix), shift_a, scale_a)
    hc = modulate(rms_norm(ctx, g_mix), cshift_a, cscale_a)
    na_q, na_k, na_v, sw_q, sw_k, sw_v, mq, mk, mv, gates = proj(hx)
    cna_q, cna_k, cna_v, csw_q, csw_k, csw_v, cmq, cmk, cmv, cgates = proj(hc)
    t = jnp.arange(S)
    row, col = t // GRID_W, t % GRID_W
    sw_q = axial_rope(sw_q, row, col)
    sw_k = axial_rope(sw_k, row, col)
    mq = jnp.concatenate([mq[..., :MLA_NOPE], axial_rope(mq[..., MLA_NOPE:], row, col)], axis=-1)
    mk = jnp.concatenate([mk[..., :MLA_NOPE], axial_rope(mk[..., MLA_NOPE:], row, col)], axis=-1)
    o_na = na_latent(na_q, na_k, na_v, cna_k, cna_v, rpb)
    o_sw = sw_latent(sw_q, sw_k, sw_v, csw_k, csw_v, sink)
    o_mla = mla_latent(mq, mk, mv, cmk, cmv)
    x = x + gate_a * merge_branches(o_na, o_sw, o_mla, gates, w_branch, w_out)
    hx = modulate(rms_norm(x, g_ffn), shift_f, scale_f)
    x = x + gate_f * conv_ffn(hx, w_up, conv_w, conv_b, w_down)
    if need_ctx:
        c_na = ctx_attend(cna_q, cna_k, cna_v)
        c_sw = ctx_attend(csw_q, csw_k, csw_v, sink)
        c_mla = ctx_attend(cmq, cmk, cmv)
        ctx = ctx + cgate_a * merge_branches(c_na, c_sw, c_mla, cgates, w_branch, w_out)
        hc = modulate(rms_norm(ctx, g_ffn), cshift_f, cscale_f)
        ctx = ctx + cgate_f * conv_ffn(hc, w_up, conv_w, conv_b, w_down)
    return x, ctx


def setup_inputs(seed: int = 0) -> dict:
    key = jax.random.key(seed)
    ks = iter(jax.random.split(key, 32))
    nrm = lambda shape, s: jax.random.normal(next(ks), shape, f32) * s
    L, D = DEPTH, D_MODEL
    gain = lambda shape: 1.0 + nrm(shape, 0.05)
    return {
        "x": nrm((BATCH, SEQ, D), 1.0),
        "c": nrm((BATCH, D), 1.0),
        "ctx": nrm((BATCH, CTX_LEN, D), 1.0),
        "c_ctx": nrm((D,), 1.0),
        "w_ada": nrm((L, D, 6 * D), 0.5 * D ** -0.5),
        "b_ada": nrm((L, 6 * D), 0.02),
        "g_mix": gain((L, D)),
        "g_ffn": gain((L, D)),
        "w_in": nrm((L, D, D_IN), D ** -0.5),
        "na_q_norm": gain((L, HEAD_DIM)),
        "na_k_norm": gain((L, HEAD_DIM)),
        "na_rpb": nrm((L, NA_HEADS, 2 * NA_WIN_ROWS - 1, 2 * NA_WIN_COLS - 1), 0.5),
        "sw_q_norm": gain((L, HEAD_DIM)),
        "sw_k_norm": gain((L, HEAD_DIM)),
        "sw_sink": nrm((L, SW_Q_HEADS), 0.5),
        "mla_q_rank_norm": gain((L, MLA_Q_RANK)),
        "mla_kv_rank_norm": gain((L, MLA_KV_RANK)),
        "w_uq": nrm((L, MLA_Q_RANK, MLA_HEADS * (MLA_NOPE + MLA_ROPE)), MLA_Q_RANK ** -0.5),
        "w_ukv": nrm((L, MLA_KV_RANK, MLA_HEADS * (MLA_NOPE + MLA_V)), MLA_KV_RANK ** -0.5),
        "mla_q_norm": gain((L, MLA_NOPE + MLA_ROPE)),
        "mla_k_norm": gain((L, MLA_NOPE + MLA_ROPE)),
        "w_branch": nrm((L, N_BRANCH, BRANCH_W, D), BRANCH_W ** -0.5),
        "w_out": nrm((L, D, D), D ** -0.5),
        "w_up": nrm((L, D, 2 * D_FF), D ** -0.5),
        "conv_w": nrm((L, CONV_W, 2 * D_FF), 0.5),
        "conv_b": nrm((L, 2 * D_FF), 0.02),
        "w_down": nrm((L, D_FF, D), D_FF ** -0.5),
    }


def reference(x, c, ctx, c_ctx, w_ada, b_ada, g_mix, g_ffn, w_in, na_q_norm, na_k_norm, na_rpb,
              sw_q_norm, sw_k_norm, sw_sink, mla_q_rank_norm, mla_kv_rank_norm, w_uq, w_ukv,
              mla_q_norm, mla_k_norm, w_branch, w_out, w_up, conv_w, conv_b, w_down):
    sc = jax.nn.silu(c)
    scc = jax.nn.silu(c_ctx)
    for l in range(DEPTH):
        mod = (sc @ w_ada[l] + b_ada[l])[:, None, :]
        mod_c = (scc @ w_ada[l] + b_ada[l])[None, None, :]
        x, ctx = trunk_layer(x, ctx, mod, mod_c, g_mix[l], g_ffn[l], w_in[l],
                             na_q_norm[l], na_k_norm[l], na_rpb[l],
                             sw_q_norm[l], sw_k_norm[l], sw_sink[l],
                             mla_q_rank_norm[l], mla_kv_rank_norm[l], w_uq[l], w_ukv[l],
                             mla_q_norm[l], mla_k_norm[l], w_branch[l], w_out[l],
                             w_up[l], conv_w[l], conv_b[l], w_down[l],
                             need_ctx=(l < DEPTH - 1))
    return x
```

```python
import functools

import jax
import jax.numpy as jnp
from jax import lax
from jax.experimental import pallas as pl
from jax.experimental.pallas import tpu as pltpu

f32 = jnp.float32
bf16 = jnp.bfloat16

GRID_W = 64
HEAD_DIM = 64
ROPE_BASE = 10000.0
NORM_EPS = 1e-6
NEG_INF = -1e30
NA_HEADS = 8
NA_WIN_ROWS = 8
NA_WIN_COLS = 16
SW_Q_HEADS = 8
SW_KV_HEADS = 2
SW_WINDOW = 128
MLA_HEADS = 8
MLA_Q_RANK = 384
MLA_KV_RANK = 256
MLA_NOPE = 64
MLA_ROPE = 32
MLA_V = 64
N_BRANCH = 3
BRANCH_W = 512
CONV_W = 3

LANES = 128
SUBLANES = 8
TM = 256
MLA_PAD = 128
FF_CHUNK = 256
NA_QROWS = TM // GRID_W
NA_KEYS = 3 * TM
VMEM_LIMIT = 56 * 1024 * 1024

_SEG = {}
_o = 0
for _name, _n in (("na_q", 512), ("na_k", 512), ("na_v", 512), ("sw_q", 512), ("sw_k", 128), ("sw_v", 128),
                  ("c_q", MLA_Q_RANK), ("c_kv", MLA_KV_RANK), ("k_r", MLA_ROPE)):
    _SEG[_name] = (_o, _o + _n)
    _o += _n
N_STREAM = _o
_GN = {}
_o = 0
for _name, _n in (("na_q", 64), ("na_k", 64), ("sw_q", 64), ("sw_k", 64), ("mla_qr", MLA_Q_RANK),
                  ("mla_kvr", MLA_KV_RANK), ("mla_q", 96), ("mla_k", 96)):
    _GN[_name] = (_o, _o + _n)
    _o += _n
N_GAIN = _o


def _dot(a, b):
    return jnp.dot(a, b, preferred_element_type=f32)


def _cparams(*sem):
    return pltpu.CompilerParams(dimension_semantics=sem, vmem_limit_bytes=VMEM_LIMIT)


def _const_spec(shape):
    nd = len(shape)
    return pl.BlockSpec(shape, lambda *_: (0,) * nd, pipeline_mode=pl.Buffered(1))


def _ada_kernel(c_ref, w_ref, b_ref, g_ref, o_ref):
    n = pl.program_id(1)
    c = c_ref[...]
    sc = (c * jax.nn.sigmoid(c)).astype(bf16)
    mod = _dot(sc, w_ref[0].astype(bf16)) + b_ref[0]
    fold = jnp.logical_or(n == 1, n == 4)
    o_ref[0] = jnp.where(fold, g_ref[0, 0] * (1.0 + mod), mod)


def _ada(c_all, w_ada, b_ada, g_sel):
    L, D, _ = w_ada.shape
    R = c_all.shape[0]
    return pl.pallas_call(
        _ada_kernel,
        grid=(L, 6),
        in_specs=[
            pl.BlockSpec((R, D), lambda l, n: (0, 0)),
            pl.BlockSpec((1, D, D), lambda l, n: (l, 0, n)),
            pl.BlockSpec((1, 1, D), lambda l, n: (l, 0, n)),
            pl.BlockSpec((1, 1, 1, D), lambda l, n: (l, n, 0, 0)),
        ],
        out_specs=pl.BlockSpec((1, R, D), lambda l, n: (l, 0, n)),
        out_shape=jax.ShapeDtypeStruct((L, R, 6 * D), f32),
        compiler_params=_cparams("parallel", "parallel"),
        name="ada",
    )(c_all, w_ada, b_ada.reshape(L, 1, 6 * D), g_sel)


def _norm_mod(x, a, s):
    ms = jnp.mean(x * x, axis=-1, keepdims=True)
    return (x * lax.rsqrt(ms + NORM_EPS)) * a + s


def _head_norm(y3, gain):
    ms = jnp.mean(y3 * y3, axis=1, keepdims=True)
    return y3 * lax.rsqrt(ms + NORM_EPS) * gain[None]


def _rope(t3, tab, lo, n):
    cr, sr, cc, sc = (tab[i * n:(i + 1) * n][None] for i in range(4))
    a, b, c, e = (t3[:, lo + i * n:lo + (i + 1) * n] for i in range(4))
    return [a * cr - b * sr, a * sr + b * cr, c * cc - e * sc, c * sc + e * cc]


def _proj_kernel(x_ref, a_ref, s_ref, win_ref, wuq_ref, wukv_ref, gn_ref, rsw_ref, rml_ref,
                 naq_ref, nak_ref, nav_ref, swq_ref, swk_ref, swv_ref, mq_ref, mk_ref, mv_ref):
    tm = x_ref.shape[1]
    h = _norm_mod(x_ref[0], a_ref[0], s_ref[0])
    ht = h.T.astype(bf16)

    def seg(name):
        lo, hi = _SEG[name]
        return _dot(win_ref[lo:hi, :], ht)

    def gain(name):
        lo, hi = _GN[name]
        return gn_ref[lo:hi, :]

    q = _head_norm(seg("na_q").reshape(NA_HEADS, HEAD_DIM, tm), gain("na_q")) * HEAD_DIM ** -0.5
    naq_ref[0] = q.reshape(NA_HEADS * HEAD_DIM, tm).astype(bf16)
    k = _head_norm(seg("na_k").reshape(NA_HEADS, HEAD_DIM, tm), gain("na_k"))
    nak_ref[0] = k.reshape(NA_HEADS * HEAD_DIM, tm).T.astype(bf16)
    nav_ref[0, 0] = seg("na_v").astype(bf16)

    rsw = rsw_ref[...]
    nq = HEAD_DIM // 4
    q = _head_norm(seg("sw_q").reshape(SW_Q_HEADS, HEAD_DIM, tm), gain("sw_q"))
    q = jnp.concatenate(_rope(q, rsw, 0, nq), axis=1) * HEAD_DIM ** -0.5
    swq_ref[0] = q.reshape(SW_Q_HEADS * HEAD_DIM, tm).astype(bf16)
    k = _head_norm(seg("sw_k").reshape(SW_KV_HEADS, HEAD_DIM, tm), gain("sw_k"))
    k = jnp.concatenate(_rope(k, rsw, 0, nq), axis=1)
    swk_ref[0] = k.reshape(SW_KV_HEADS * HEAD_DIM, tm).T.astype(bf16)
    v = seg("sw_v").astype(bf16)
    for j in range(tm // LANES):
        swv_ref[0, j] = v[:, j * LANES:(j + 1) * LANES]

    rml = rml_ref[...]
    nr = MLA_ROPE // 4
    dq = MLA_NOPE + MLA_ROPE
    pad = jnp.zeros((MLA_HEADS, MLA_PAD - dq, tm), f32)
    cq = seg("c_q")
    cq = cq * lax.rsqrt(jnp.mean(cq * cq, axis=0, keepdims=True) + NORM_EPS) * gain("mla_qr")
    mq = _head_norm(_dot(wuq_ref[...], cq.astype(bf16)).reshape(MLA_HEADS, dq, tm), gain("mla_q"))
    mq = jnp.concatenate([mq[:, :MLA_NOPE]] + _rope(mq, rml, MLA_NOPE, nr), axis=1) * dq ** -0.5
    mq_ref[0] = jnp.concatenate([mq, pad], axis=1).reshape(MLA_HEADS * MLA_PAD, tm).astype(bf16)

    ckv = seg("c_kv")
    ckv = ckv * lax.rsqrt(jnp.mean(ckv * ckv, axis=0, keepdims=True) + NORM_EPS) * gain("mla_kvr")
    kv = _dot(wukv_ref[...], ckv.astype(bf16)).reshape(MLA_HEADS, MLA_NOPE + MLA_V, tm)
    kn, v = kv[:, :MLA_NOPE], kv[:, MLA_NOPE:]
    kr = seg("k_r")
    ss = jnp.sum(kn * kn, axis=1, keepdims=True) + jnp.sum(kr * kr, axis=0, keepdims=True)[None]
    r = lax.rsqrt(ss / dq + NORM_EPS)
    gk = gain("mla_k")
    kn = kn * r * gk[:MLA_NOPE][None]
    krh = kr[None] * r * gk[MLA_NOPE:][None]
    mk = jnp.concatenate([kn] + _rope(krh, rml, 0, nr) + [pad], axis=1)
    mk_ref[0] = mk.reshape(MLA_HEADS * MLA_PAD, tm).T.astype(bf16)
    mv_ref[0, 0] = v.reshape(MLA_HEADS * MLA_V, tm).astype(bf16)


def _proj(xc, a_row, s_row, win_t, wuq_t, wukv_t, gains, rope_sw, rope_ml):
    B, Lc, D = xc.shape
    T = Lc // TM
    nb = a_row.shape[0] - 1
    row = lambda b, t: (jnp.where(t == 0, nb, b), 0, 0)
    fm = lambda n: pl.BlockSpec((1, n, TM), lambda b, t: (b, 0, t))
    tk = lambda n: pl.BlockSpec((1, TM, n), lambda b, t: (b, t, 0))
    sds = jax.ShapeDtypeStruct
    return pl.pallas_call(
        _proj_kernel,
        grid=(B, T),
        in_specs=[
            pl.BlockSpec((1, TM, D), lambda b, t: (b, t, 0)),
            pl.BlockSpec((1, 1, D), row),
            pl.BlockSpec((1, 1, D), row),
            _const_spec(win_t.shape), _const_spec(wuq_t.shape), _const_spec(wukv_t.shape),
            _const_spec(gains.shape),
            pl.BlockSpec((rope_sw.shape[0], TM), lambda b, t: (0, t)),
            pl.BlockSpec((rope_ml.shape[0], TM), lambda b, t: (0, t)),
        ],
        out_specs=[
            fm(512), tk(512), pl.BlockSpec((1, 1, 512, TM), lambda b, t: (b, t, 0, 0)),
            fm(512), tk(128), pl.BlockSpec((1, TM // LANES, 128, LANES), lambda b, t: (b, t, 0, 0)),
            fm(MLA_HEADS * MLA_PAD), tk(MLA_HEADS * MLA_PAD),
            pl.BlockSpec((1, 1, 512, TM), lambda b, t: (b, t, 0, 0)),
        ],
        out_shape=[
            sds((B, 512, Lc), bf16), sds((B, Lc, 512), bf16), sds((B, T, 512, TM), bf16),
            sds((B, 512, Lc), bf16), sds((B, Lc, 128), bf16), sds((B, Lc // LANES, 128, LANES), bf16),
            sds((B, MLA_HEADS * MLA_PAD, Lc), bf16), sds((B, Lc, MLA_HEADS * MLA_PAD), bf16),
            sds((B, T, 512, TM), bf16),
        ],
        compiler_params=_cparams("parallel", "parallel"),
        name="proj",
    )(xc, a_row, s_row, win_t, wuq_t, wukv_t, gains, rope_sw, rope_ml)


def _na_kernel(q_ref, kc_ref, k0_ref, k1_ref, k2_ref, vc_ref, v0_ref, v1_ref, v2_ref, bias_ref, o_ref, acc_ref):
    tq = q_ref.shape[2]
    zeros = jnp.zeros((HEAD_DIM, tq), bf16)
    k_refs = (kc_ref, k0_ref, k1_ref, k2_ref)
    v_refs = (vc_ref, v0_ref, v1_ref, v2_ref)
    for h in range(NA_HEADS):
        rows = slice(h * HEAD_DIM, (h + 1) * HEAD_DIM)
        pair = slice((h // 2) * LANES, (h // 2 + 1) * LANES)
        qh = q_ref[0, rows, :]
        qpad = jnp.concatenate([qh, zeros] if h % 2 == 0 else [zeros, qh], axis=0)
        s = [_dot(kr[0, :, pair], qpad) for kr in k_refs]
        for i in range(3):
            s[i + 1] = s[i + 1] + bias_ref[0, h, i * TM:(i + 1) * TM, :]
        m = functools.reduce(jnp.maximum, [jnp.max(x, axis=0, keepdims=True) for x in s])
        p = [jnp.exp(x - m) for x in s]
        l = functools.reduce(jnp.add, [jnp.sum(x, axis=0, keepdims=True) for x in p])
        o = functools.reduce(jnp.add, [_dot(vr[0, 0, rows, :], x.astype(bf16)) for vr, x in zip(v_refs, p)])
        acc_ref[rows, :] = o / l
    o_ref[0] = acc_ref[...].T.astype(bf16)


def _na(q_t, k, v_c, bias, off):
    B, _, Lc = q_t.shape
    T = Lc // TM
    nq = T - off
    nblk = T - 1

    def base(i):
        return 1 + jnp.clip(i + off - 2, 0, nblk - 3)

    def btype(i):
        qt = i + off
        return jnp.where(qt == 0, 3, jnp.where(qt == 1, 0, jnp.where(qt == T - 1, 2, 1)))

    kspec = lambda j: pl.BlockSpec((1, TM, 512), lambda b, i: (b, base(i) + j, 0))
    vspec = lambda j: pl.BlockSpec((1, 1, 512, TM), lambda b, i: (b, base(i) + j, 0, 0))
    return pl.pallas_call(
        _na_kernel,
        grid=(B, nq),
        in_specs=[
            pl.BlockSpec((1, 512, TM), lambda b, i: (b, 0, i + off)),
            pl.BlockSpec((1, TM, 512), lambda b, i: (b, 0, 0)), kspec(0), kspec(1), kspec(2),
            pl.BlockSpec((1, 1, 512, TM), lambda b, i: (b, 0, 0, 0)), vspec(0), vspec(1), vspec(2),
            pl.BlockSpec((1, NA_HEADS, NA_KEYS, TM), lambda b, i: (btype(i), 0, 0, 0)),
        ],
        out_specs=pl.BlockSpec((1, TM, 512), lambda b, i: (b, i, 0)),
        out_shape=jax.ShapeDtypeStruct((B, nq * TM, 512), bf16),
        scratch_shapes=[pltpu.VMEM((512, TM), f32)],
        compiler_params=_cparams("parallel", "arbitrary"),
        name="na_attn",
    )(q_t, k, k, k, k, v_c, v_c, v_c, v_c, bias)


def _na_bias_table(rpb, rows):
    def one(r0, ks):
        kk = jnp.arange(NA_KEYS)
        kr, kc = ks + kk // GRID_W, kk % GRID_W
        qq = jnp.arange(TM)
        qr, qc = r0 + qq // GRID_W, qq % GRID_W
        start = jnp.clip(qr - NA_WIN_ROWS // 2, 0, rows - NA_WIN_ROWS)
        rv = (kr[:, None] >= start[None, :]) & (kr[:, None] < start[None, :] + NA_WIN_ROWS)
        c0 = jnp.clip(qc - NA_WIN_COLS // 2, 0, GRID_W - NA_WIN_COLS)
        cv = (kc[:, None] >= c0[None, :]) & (kc[:, None] < c0[None, :] + NA_WIN_COLS)
        dr = jnp.clip(kr[:, None] - qr[None, :] + (NA_WIN_ROWS - 1), 0, 2 * NA_WIN_ROWS - 2)
        dc = jnp.clip(kc[:, None] - qc[None, :], -(NA_WIN_COLS - 1), NA_WIN_COLS - 1) + (NA_WIN_COLS - 1)
        return jnp.where((rv & cv)[None], rpb[:, dr, dc], NEG_INF)

    krows = NA_KEYS // GRID_W
    tabs = [one(0, 0), one(NA_QROWS, 0), one(rows - NA_QROWS, rows - krows)]
    tabs.append(jnp.full_like(tabs[0], NEG_INF))
    return jnp.stack(tabs).astype(f32)


def _sw_kernel(q_ref, kc_ref, k0_ref, k1_ref, k2_ref, k3_ref, vc_ref, v0_ref, v1_ref, v2_ref, v3_ref,
               sink_ref, o_ref, acc_ref, *, off, ctx_len, total_len):
    tq = q_ref.shape[2]
    qt = pl.program_id(1) + off
    group = SW_Q_HEADS // SW_KV_HEADS
    k_refs = (k0_ref, k1_ref, k2_ref, k3_ref)
    v_refs = (v0_ref, v1_ref, v2_ref, v3_ref)
    kk = lax.broadcasted_iota(jnp.int32, (LANES, tq), 0)
    qpos = qt * tq + lax.broadcasted_iota(jnp.int32, (LANES, tq), 1)
    madd = []
    for i in range(4):
        kpos = (qt * (tq // LANES) - 1 + i) * LANES + kk
        ok = (kpos >= ctx_len) & (kpos < total_len) & (jnp.abs(kpos - qpos) <= SW_WINDOW) & (qt > 0)
        m1 = jnp.where(ok, 0.0, NEG_INF).astype(f32)
        madd.append(jnp.concatenate([m1] * group, axis=1))
    for g in range(SW_KV_HEADS):
        kv_rows = slice(g * HEAD_DIM, (g + 1) * HEAD_DIM)
        qg = jnp.concatenate([q_ref[0, (group * g + hh) * HEAD_DIM:(group * g + hh + 1) * HEAD_DIM, :]
                              for hh in range(group)], axis=1)
        zeros = jnp.zeros_like(qg)
        qpad = jnp.concatenate([qg, zeros] if g == 0 else [zeros, qg], axis=0)
        s = [_dot(kc_ref[0], qpad)] + [_dot(kr[0], qpad) + madd[i] for i, kr in enumerate(k_refs)]
        sink = sink_ref[g]
        m = functools.reduce(jnp.maximum, [jnp.max(x, axis=0, keepdims=True) for x in s] + [sink])
        p = [jnp.exp(x - m) for x in s]
        l = functools.reduce(jnp.add, [jnp.sum(x, axis=0, keepdims=True) for x in p] + [jnp.exp(sink - m)])
        pc = p[0].astype(bf16)
        parts = [_dot(vc_ref[0, j, kv_rows, :], pc[j * LANES:(j + 1) * LANES]) for j in range(tq // LANES)]
        parts += [_dot(vr[0, 0, kv_rows, :], x.astype(bf16)) for vr, x in zip(v_refs, p[1:])]
        o = functools.reduce(jnp.add, parts) / l
        for hh in range(group):
            h = group * g + hh
            acc_ref[h * HEAD_DIM:(h + 1) * HEAD_DIM, :] = o[:, hh * tq:(hh + 1) * tq]
    o_ref[0] = acc_ref[...].T.astype(bf16)


def _sw(q_t, k, v_c, sink_rows, off, ctx_len):
    B, _, Lc = q_t.shape
    T = Lc // TM
    nq = T - off
    per = TM // LANES
    lo, hi = ctx_len // LANES, Lc // LANES - 1

    def blk(i, j):
        return jnp.clip((i + off) * per - 1 + j, lo, hi)

    kspec = lambda j: pl.BlockSpec((1, LANES, 128), lambda b, i: (b, blk(i, j), 0))
    vspec = lambda j: pl.BlockSpec((1, 1, 128, LANES), lambda b, i: (b, blk(i, j), 0, 0))
    kern = functools.partial(_sw_kernel, off=off, ctx_len=ctx_len, total_len=Lc)
    return pl.pallas_call(
        kern,
        grid=(B, nq),
        in_specs=[
            pl.BlockSpec((1, 512, TM), lambda b, i: (b, 0, i + off)),
            pl.BlockSpec((1, ctx_len, 128), lambda b, i: (b, 0, 0)),
            kspec(0), kspec(1), kspec(2), kspec(3),
            pl.BlockSpec((1, ctx_len // LANES, 128, LANES), lambda b, i: (b, 0, 0, 0)),
            vspec(0), vspec(1), vspec(2), vspec(3),
            pl.BlockSpec(sink_rows.shape, lambda b, i: (0, 0, 0)),
        ],
        out_specs=pl.BlockSpec((1, TM, 512), lambda b, i: (b, i, 0)),
        out_shape=jax.ShapeDtypeStruct((B, nq * TM, 512), bf16),
        scratch_shapes=[pltpu.VMEM((512, TM), f32)],
        compiler_params=_cparams("parallel", "arbitrary"),
        name="sw_attn",
    )(q_t, k, k, k, k, k, v_c, v_c, v_c, v_c, v_c, sink_rows)


def _mla_kernel(q_ref, k_ref, v_ref, o_ref, acc_ref, *, off):
    tq = q_ref.shape[2]
    nchunk = v_ref.shape[1]
    qt = pl.program_id(1) + off
    nk = jnp.where(qt == 0, 1, nchunk)
    for h in range(MLA_HEADS):
        qh = q_ref[0, h * MLA_PAD:(h + 1) * MLA_PAD, :]
        cols = slice(h * MLA_PAD, (h + 1) * MLA_PAD)
        rows = slice(h * MLA_V, (h + 1) * MLA_V)

        def body(j, carry):
            m, l, acc = carry
            kj = k_ref[0, pl.ds(pl.multiple_of(j * TM, TM), TM), cols]
            s = _dot(kj, qh)
            m_new = jnp.maximum(m, jnp.max(s, axis=0, keepdims=True))
            alpha = jnp.exp(m - m_new)
            p = jnp.exp(s - m_new)
            l = alpha * l + jnp.sum(p, axis=0, keepdims=True)
            acc = alpha * acc + _dot(v_ref[0, j, rows, :], p.astype(bf16))
            return m_new, l, acc

        init = (jnp.full((1, tq), NEG_INF, f32), jnp.zeros((1, tq), f32), jnp.zeros((MLA_V, tq), f32))
        _, l, acc = lax.fori_loop(0, nk, body, init)
        acc_ref[rows, :] = acc / l
    o_ref[0] = acc_ref[...].T.astype(bf16)


def _mla(q_t, k, v_c, off):
    B, _, Lc = q_t.shape
    T = Lc // TM
    nq = T - off
    return pl.pallas_call(
        functools.partial(_mla_kernel, off=off),
        grid=(B, nq),
        in_specs=[
            pl.BlockSpec((1, MLA_HEADS * MLA_PAD, TM), lambda b, i: (b, 0, i + off)),
            pl.BlockSpec((1, Lc, MLA_HEADS * MLA_PAD), lambda b, i: (b, 0, 0)),
            pl.BlockSpec((1, T, 512, TM), lambda b, i: (b, 0, 0, 0)),
        ],
        out_specs=pl.BlockSpec((1, TM, 512), lambda b, i: (b, i, 0)),
        out_shape=jax.ShapeDtypeStruct((B, nq * TM, 512), bf16),
        scratch_shapes=[pltpu.VMEM((512, TM), f32)],
        compiler_params=_cparams("parallel", "arbitrary"),
        name="mla_attn",
    )(q_t, k, v_c)


def _merge_kernel(x_ref, a_ref, s_ref, ga_ref, ona_ref, osw_ref, oml_ref, wg_ref, wb_ref, wo_ref, o_ref):
    x = x_ref[0]
    d = x.shape[1]
    hb = _norm_mod(x, a_ref[0], s_ref[0]).astype(bf16)
    z = None
    for n, o_n in enumerate((ona_ref, osw_ref, oml_ref)):
        g = jax.nn.sigmoid(_dot(hb, wg_ref[:, n * d:(n + 1) * d]))
        y = g * _dot(o_n[0], wb_ref[n])
        z = y if z is None else z + y
    o_ref[0] = x + ga_ref[0] * _dot(z.astype(bf16), wo_ref[...])


def _merge(xc, a_row, s_row, g_row, o_na, o_sw, o_ml, w_gate, w_branch, w_out, off):
    B, Lc, D = xc.shape
    nt = Lc // TM - off
    nb = a_row.shape[0] - 1
    row = lambda b, i: (jnp.where(i + off == 0, nb, b), 0, 0)
    ospec = pl.BlockSpec((1, TM, BRANCH_W), lambda b, i: (b, i, 0))
    return pl.pallas_call(
        _merge_kernel,
        grid=(B, nt),
        in_specs=[
            pl.BlockSpec((1, TM, D), lambda b, i: (b, i + off, 0)),
            pl.BlockSpec((1, 1, D), row), pl.BlockSpec((1, 1, D), row), pl.BlockSpec((1, 1, D), row),
            ospec, ospec, ospec,
            _const_spec(w_gate.shape), _const_spec(w_branch.shape), _const_spec(w_out.shape),
        ],
        out_specs=pl.BlockSpec((1, TM, D), lambda b, i: (b, i, 0)),
        out_shape=jax.ShapeDtypeStruct((B, nt * TM, D), f32),
        compiler_params=_cparams("parallel", "parallel"),
        name="merge",
    )(xc, a_row, s_row, g_row, o_na, o_sw, o_ml, w_gate, w_branch, w_out)


def _ffn_kernel(x_ref, xp_ref, xn_ref, a_ref, s_ref, gf_ref, wup_ref, cw_ref, cb_ref, wdn_ref, o_ref, acc_ref,
                *, ctx_tiles):
    tm = x_ref.shape[1]
    halo = xp_ref.shape[1]
    t = pl.program_id(1)
    nt = pl.num_programs(1)
    x = x_ref[0]
    xe = jnp.concatenate([xp_ref[0], x, xn_ref[0]], axis=0)
    hb = _norm_mod(xe, a_ref[0], s_ref[0]).astype(bf16)
    left_ok = jnp.where(jnp.logical_and(t != 0, t != ctx_tiles), 1.0, 0.0)
    right_ok = jnp.where(jnp.logical_and(t != nt - 1, t != ctx_tiles - 1), 1.0, 0.0)
    ri = lax.broadcasted_iota(jnp.int32, (tm + 2 * halo, 1), 0)
    keep = jnp.where(ri < halo, left_ok, jnp.where(ri >= tm + halo, right_ok, 1.0)).astype(f32)
    acc_ref[...] = jnp.zeros_like(acc_ref)

    def chunk(c, carry):
        u = _dot(hb, wup_ref[c]) * keep
        w = cw_ref[c]
        uc = (w[0:1] * u[halo - 1:halo - 1 + tm] + w[1:2] * u[halo:halo + tm]
              + w[2:3] * u[halo + 1:halo + 1 + tm] + cb_ref[c])
        g, v = uc[:, :FF_CHUNK], uc[:, FF_CHUNK:]
        act = (g * jax.nn.sigmoid(g) * v).astype(bf16)
        acc_ref[...] += _dot(act, wdn_ref[c])
        return carry

    lax.fori_loop(0, wup_ref.shape[0], chunk, 0)
    o_ref[0] = x + gf_ref[0] * acc_ref[...]


def _ffn(x1, a_row, s_row, g_row, w_up_c, conv_w_c, conv_b_c, w_down_c, ctx_tiles):
    B, Lt, D = x1.shape
    nt = Lt // TM
    nb = a_row.shape[0] - 1
    per = TM // SUBLANES
    last = Lt // SUBLANES - 1
    row = lambda b, t: (jnp.where(t < ctx_tiles, nb, b), 0, 0)
    return pl.pallas_call(
        functools.partial(_ffn_kernel, ctx_tiles=ctx_tiles),
        grid=(B, nt),
        in_specs=[
            pl.BlockSpec((1, TM, D), lambda b, t: (b, t, 0)),
            pl.BlockSpec((1, SUBLANES, D), lambda b, t: (b, jnp.maximum(t * per - 1, 0), 0)),
            pl.BlockSpec((1, SUBLANES, D), lambda b, t: (b, jnp.minimum((t + 1) * per, last), 0)),
            pl.BlockSpec((1, 1, D), row), pl.BlockSpec((1, 1, D), row), pl.BlockSpec((1, 1, D), row),
            _const_spec(w_up_c.shape), _const_spec(conv_w_c.shape), _const_spec(conv_b_c.shape),
            _const_spec(w_down_c.shape),
        ],
        out_specs=pl.BlockSpec((1, TM, D), lambda b, t: (b, t, 0)),
        out_shape=jax.ShapeDtypeStruct((B, Lt, D), f32),
        scratch_shapes=[pltpu.VMEM((TM, D), f32)],
        compiler_params=_cparams("parallel", "arbitrary"),
        name="ffn",
    )(x1, x1, x1, a_row, s_row, g_row, w_up_c, conv_w_c, conv_b_c, w_down_c)


def _rope_table(seq, ctx_len, n):
    t = jnp.arange(seq)
    inv = ROPE_BASE ** (-jnp.arange(n, dtype=f32) / n)
    ar = (t // GRID_W).astype(f32)[None, :] * inv[:, None]
    ac = (t % GRID_W).astype(f32)[None, :] * inv[:, None]
    lat = jnp.concatenate([jnp.cos(ar), jnp.sin(ar), jnp.cos(ac), jnp.sin(ac)], axis=0)
    one, zero = jnp.ones((n, ctx_len), f32), jnp.zeros((n, ctx_len), f32)
    return jnp.concatenate([jnp.concatenate([one, zero, one, zero], axis=0), lat], axis=1)


def _chunked_ffn_weights(w_up, conv_w, conv_b, w_down):
    d_ff = w_down.shape[0]
    nc = d_ff // FF_CHUNK
    pair = lambda a: jnp.concatenate([a[..., :d_ff].reshape(a.shape[:-1] + (nc, FF_CHUNK)),
                                      a[..., d_ff:].reshape(a.shape[:-1] + (nc, FF_CHUNK))], axis=-1)
    w_up_c = jnp.moveaxis(pair(w_up), 1, 0).astype(bf16)
    conv_w_c = jnp.moveaxis(pair(conv_w), 1, 0)
    conv_b_c = pair(conv_b)[:, None, :]
    w_down_c = w_down.reshape(nc, FF_CHUNK, w_down.shape[1]).astype(bf16)
    return w_up_c, conv_w_c, conv_b_c, w_down_c


def kernel(x, c, ctx, c_ctx, w_ada, b_ada, g_mix, g_ffn, w_in, na_q_norm, na_k_norm, na_rpb, sw_q_norm, sw_k_norm,
           sw_sink, mla_q_rank_norm, mla_kv_rank_norm, w_uq, w_ukv, mla_q_norm, mla_k_norm, w_branch, w_out,
           w_up, conv_w, conv_b, w_down):
    B, S, D = x.shape
    C = ctx.shape[1]
    depth = w_ada.shape[0]
    assert C == TM and S % TM == 0 and GRID_W * NA_QROWS == TM and (S // GRID_W) * GRID_W == S
    assert S // GRID_W >= NA_KEYS // GRID_W and w_down.shape[1] % FF_CHUNK == 0
    rows = S // GRID_W

    xc = jnp.concatenate([ctx, x], axis=1)
    n_mod = -(-(B + 1) // SUBLANES) * SUBLANES
    c_all = jnp.zeros((n_mod, D), f32).at[:B].set(c).at[B].set(c_ctx)
    zero = jnp.zeros_like(g_mix)
    g_sel = jnp.stack([zero, g_mix, zero, zero, g_ffn, zero], axis=1)[:, :, None, :]
    mod = _ada(c_all, w_ada, b_ada, g_sel)

    rope_sw = _rope_table(S, C, HEAD_DIM // 4)
    rope_ml = _rope_table(S, C, MLA_ROPE // 4)
    group = SW_Q_HEADS // SW_KV_HEADS

    for l in range(depth):
        off = 0 if l < depth - 1 else 1
        m6 = mod[l, :B + 1].reshape(B + 1, 6, 1, D)
        shift_a, a_mix, gate_a, shift_f, a_ffn, gate_f = (m6[:, j] for j in range(6))
        w_in_t = w_in[l][:, :N_STREAM].T.astype(bf16)
        gains = jnp.concatenate([na_q_norm[l], na_k_norm[l], sw_q_norm[l], sw_k_norm[l], mla_q_rank_norm[l],
                                 mla_kv_rank_norm[l], mla_q_norm[l], mla_k_norm[l]])[:, None]
        (na_q, na_k, na_v, sw_q, sw_k, sw_v, m_q, m_k, m_v) = _proj(
            xc, a_mix, shift_a, w_in_t, w_uq[l].T.astype(bf16), w_ukv[l].T.astype(bf16),
            gains, rope_sw, rope_ml)
        o_na = _na(na_q, na_k, na_v, _na_bias_table(na_rpb[l], rows), off)
        sink_rows = jnp.repeat(sw_sink[l].reshape(SW_KV_HEADS, 1, group), TM, axis=2)
        o_sw = _sw(sw_q, sw_k, sw_v, sink_rows, off, C)
        o_ml = _mla(m_q, m_k, m_v, off)
        x1 = _merge(xc, a_mix, shift_a, gate_a, o_na, o_sw, o_ml, w_in[l][:, N_STREAM:].astype(bf16),
                    w_branch[l].astype(bf16), w_out[l].astype(bf16), off)
        xc = _ffn(x1, a_ffn, shift_f, gate_f, *_chunked_ffn_weights(w_up[l], conv_w[l], conv_b[l], w_down[l]),
                  ctx_tiles=1 - off)
    return xc
```

```python
import functools

import jax
import jax.numpy as jnp
from jax import lax
from jax.experimental import pallas as pl
from jax.experimental.pallas import tpu as pltpu

f32 = jnp.float32
bf16 = jnp.bfloat16

GRID_W = 64
HEAD_DIM = 64
ROPE_BASE = 10000.0
NORM_EPS = 1e-6
NEG_INF = -1e30
NA_HEADS = 8
NA_WIN_ROWS = 8
NA_WIN_COLS = 16
SW_Q_HEADS = 8
SW_KV_HEADS = 2
SW_WINDOW = 128
MLA_HEADS = 8
MLA_Q_RANK = 384
MLA_KV_RANK = 256
MLA_NOPE = 64
MLA_ROPE = 32
MLA_V = 64
N_BRANCH = 3
BRANCH_W = 512
CONV_W = 3

LANES = 128
SUBLANES = 8
TM = 256
MLA_PAD = 128
MLA_STEP_CHUNKS = 2
FF_CHUNK = 256
NA_QROWS = TM // GRID_W
NA_KEYS = 3 * TM
BF16_ROWS = 16
V_EXT = HEAD_DIM + BF16_ROWS
LOG2E = 1.4426950408889634
VMEM_LIMIT = 56 * 1024 * 1024

_SEG = {}
_o = 0
for _name, _n in (("na_q", 512), ("na_k", 512), ("na_v", 512), ("sw_q", 512), ("sw_k", 128), ("sw_v", 128),
                  ("c_q", MLA_Q_RANK), ("c_kv", MLA_KV_RANK), ("k_r", MLA_ROPE)):
    _SEG[_name] = (_o, _o + _n)
    _o += _n
N_STREAM = _o
_GN = {}
_o = 0
for _name, _n in (("na_q", 64), ("na_k", 64), ("sw_q", 64), ("sw_k", 64), ("mla_qr", MLA_Q_RANK),
                  ("mla_kvr", MLA_KV_RANK), ("mla_q", 96), ("mla_k", 96)):
    _GN[_name] = (_o, _o + _n)
    _o += _n
N_GAIN = _o


def _dot(a, b):
    return jnp.dot(a, b, preferred_element_type=f32)


def _cparams(*sem):
    return pltpu.CompilerParams(dimension_semantics=sem, vmem_limit_bytes=VMEM_LIMIT)


def _const_spec(shape):
    nd = len(shape)
    return pl.BlockSpec(shape, lambda *_: (0,) * nd, pipeline_mode=pl.Buffered(1))


def _ada_kernel(c_ref, w_ref, b_ref, g_ref, o_ref):
    n = pl.program_id(1)
    c = c_ref[...]
    sc = (c * jax.nn.sigmoid(c)).astype(bf16)
    mod = _dot(sc, w_ref[0].astype(bf16)) + b_ref[0]
    fold = jnp.logical_or(n == 1, n == 4)
    o_ref[0] = jnp.where(fold, g_ref[0, 0] * (1.0 + mod), mod)


def _ada(c_all, w_ada, b_ada, g_sel):
    L, D, _ = w_ada.shape
    R = c_all.shape[0]
    return pl.pallas_call(
        _ada_kernel,
        grid=(L, 6),
        in_specs=[
            pl.BlockSpec((R, D), lambda l, n: (0, 0)),
            pl.BlockSpec((1, D, D), lambda l, n: (l, 0, n)),
            pl.BlockSpec((1, 1, D), lambda l, n: (l, 0, n)),
            pl.BlockSpec((1, 1, 1, D), lambda l, n: (l, n, 0, 0)),
        ],
        out_specs=pl.BlockSpec((1, R, D), lambda l, n: (l, 0, n)),
        out_shape=jax.ShapeDtypeStruct((L, R, 6 * D), f32),
        compiler_params=_cparams("parallel", "parallel"),
        name="ada",
    )(c_all, w_ada, b_ada.reshape(L, 1, 6 * D), g_sel)


def _norm_mod(x, a, s):
    ms = jnp.mean(x * x, axis=-1, keepdims=True)
    return (x * lax.rsqrt(ms + NORM_EPS)) * a + s


def _head_norm(y3, gain):
    ms = jnp.mean(y3 * y3, axis=1, keepdims=True)
    return y3 * lax.rsqrt(ms + NORM_EPS) * gain[None]


def _rope(t3, tab, lo, n):
    cr, sr, cc, sc = (tab[i * n:(i + 1) * n][None] for i in range(4))
    a, b, c, e = (t3[:, lo + i * n:lo + (i + 1) * n] for i in range(4))
    return [a * cr - b * sr, a * sr + b * cr, c * cc - e * sc, c * sc + e * cc]


def _proj_kernel(x_ref, a_ref, s_ref, win_ref, wuq_ref, wukv_ref, gn_ref, rsw_ref, rml_ref,
                 naq_ref, nak_ref, nav_ref, swq_ref, swk_ref, swv_ref, mq_ref, mk_ref, mv_ref):
    tm = x_ref.shape[1]
    h = _norm_mod(x_ref[0], a_ref[0], s_ref[0])
    ht = h.T.astype(bf16)

    def seg(name):
        lo, hi = _SEG[name]
        return _dot(win_ref[lo:hi, :], ht)

    def gain(name):
        lo, hi = _GN[name]
        return gn_ref[lo:hi, :]

    def with_ones(v, heads):
        v3 = v.reshape(heads, HEAD_DIM, tm)
        ones = jnp.ones((heads, BF16_ROWS, tm), f32)
        return jnp.concatenate([v3, ones], axis=1).reshape(heads * V_EXT, tm).astype(bf16)

    qs = HEAD_DIM ** -0.5 * LOG2E
    q = _head_norm(seg("na_q").reshape(NA_HEADS, HEAD_DIM, tm), gain("na_q")) * qs
    naq_ref[0] = q.reshape(NA_HEADS * HEAD_DIM, tm).astype(bf16)
    k = _head_norm(seg("na_k").reshape(NA_HEADS, HEAD_DIM, tm), gain("na_k"))
    nak_ref[0] = k.reshape(NA_HEADS * HEAD_DIM, tm).T.astype(bf16)
    nav_ref[0, 0] = with_ones(seg("na_v"), NA_HEADS)

    rsw = rsw_ref[...]
    nq = HEAD_DIM // 4
    q = _head_norm(seg("sw_q").reshape(SW_Q_HEADS, HEAD_DIM, tm), gain("sw_q"))
    q = jnp.concatenate(_rope(q, rsw, 0, nq), axis=1) * qs
    swq_ref[0] = q.reshape(SW_Q_HEADS * HEAD_DIM, tm).astype(bf16)
    k = _head_norm(seg("sw_k").reshape(SW_KV_HEADS, HEAD_DIM, tm), gain("sw_k"))
    k = jnp.concatenate(_rope(k, rsw, 0, nq), axis=1)
    swk_ref[0] = k.reshape(SW_KV_HEADS * HEAD_DIM, tm).T.astype(bf16)
    v = with_ones(seg("sw_v"), SW_KV_HEADS)
    for j in range(tm // LANES):
        swv_ref[0, j] = v[:, j * LANES:(j + 1) * LANES]

    rml = rml_ref[...]
    nr = MLA_ROPE // 4
    dq = MLA_NOPE + MLA_ROPE
    pad = jnp.zeros((MLA_HEADS, MLA_PAD - dq, tm), f32)
    cq = seg("c_q")
    cq = cq * lax.rsqrt(jnp.mean(cq * cq, axis=0, keepdims=True) + NORM_EPS) * gain("mla_qr")
    mq = _head_norm(_dot(wuq_ref[...], cq.astype(bf16)).reshape(MLA_HEADS, dq, tm), gain("mla_q"))
    mq = jnp.concatenate([mq[:, :MLA_NOPE]] + _rope(mq, rml, MLA_NOPE, nr), axis=1) * (dq ** -0.5 * LOG2E)
    mq_ref[0] = jnp.concatenate([mq, pad], axis=1).reshape(MLA_HEADS * MLA_PAD, tm).astype(bf16)

    ckv = seg("c_kv")
    ckv = ckv * lax.rsqrt(jnp.mean(ckv * ckv, axis=0, keepdims=True) + NORM_EPS) * gain("mla_kvr")
    kv = _dot(wukv_ref[...], ckv.astype(bf16)).reshape(MLA_HEADS, MLA_NOPE + MLA_V, tm)
    kn, v = kv[:, :MLA_NOPE], kv[:, MLA_NOPE:]
    kr = seg("k_r")
    ss = jnp.sum(kn * kn, axis=1, keepdims=True) + jnp.sum(kr * kr, axis=0, keepdims=True)[None]
    r = lax.rsqrt(ss / dq + NORM_EPS)
    gk = gain("mla_k")
    kn = kn * r * gk[:MLA_NOPE][None]
    krh = kr[None] * r * gk[MLA_NOPE:][None]
    mk = jnp.concatenate([kn] + _rope(krh, rml, 0, nr) + [pad], axis=1)
    mk_ref[0] = mk.reshape(MLA_HEADS * MLA_PAD, tm).T.astype(bf16)
    mv_ref[0, 0] = with_ones(v.reshape(MLA_HEADS * MLA_V, tm), MLA_HEADS)


def _proj(xc, a_row, s_row, win_t, wuq_t, wukv_t, gains, rope_sw, rope_ml):
    B, Lc, D = xc.shape
    T = Lc // TM
    nb = a_row.shape[0] - 1
    row = lambda b, t: (jnp.where(t == 0, nb, b), 0, 0)
    fm = lambda n: pl.BlockSpec((1, n, TM), lambda b, t: (b, 0, t))
    tk = lambda n: pl.BlockSpec((1, TM, n), lambda b, t: (b, t, 0))
    sds = jax.ShapeDtypeStruct
    return pl.pallas_call(
        _proj_kernel,
        grid=(B, T),
        in_specs=[
            pl.BlockSpec((1, TM, D), lambda b, t: (b, t, 0)),
            pl.BlockSpec((1, 1, D), row),
            pl.BlockSpec((1, 1, D), row),
            _const_spec(win_t.shape), _const_spec(wuq_t.shape), _const_spec(wukv_t.shape),
            _const_spec(gains.shape),
            pl.BlockSpec((rope_sw.shape[0], TM), lambda b, t: (0, t)),
            pl.BlockSpec((rope_ml.shape[0], TM), lambda b, t: (0, t)),
        ],
        out_specs=[
            fm(512), tk(512), pl.BlockSpec((1, 1, NA_HEADS * V_EXT, TM), lambda b, t: (b, t, 0, 0)),
            fm(512), tk(128),
            pl.BlockSpec((1, TM // LANES, SW_KV_HEADS * V_EXT, LANES), lambda b, t: (b, t, 0, 0)),
            fm(MLA_HEADS * MLA_PAD), tk(MLA_HEADS * MLA_PAD),
            pl.BlockSpec((1, 1, MLA_HEADS * V_EXT, TM), lambda b, t: (b, t, 0, 0)),
        ],
        out_shape=[
            sds((B, 512, Lc), bf16), sds((B, Lc, 512), bf16), sds((B, T, NA_HEADS * V_EXT, TM), bf16),
            sds((B, 512, Lc), bf16), sds((B, Lc, 128), bf16),
            sds((B, Lc // LANES, SW_KV_HEADS * V_EXT, LANES), bf16),
            sds((B, MLA_HEADS * MLA_PAD, Lc), bf16), sds((B, Lc, MLA_HEADS * MLA_PAD), bf16),
            sds((B, T, MLA_HEADS * V_EXT, TM), bf16),
        ],
        compiler_params=_cparams("parallel", "parallel"),
        name="proj",
    )(xc, a_row, s_row, win_t, wuq_t, wukv_t, gains, rope_sw, rope_ml)


def _na_kernel(q_ref, kc_ref, k0_ref, k1_ref, k2_ref, vc_ref, v0_ref, v1_ref, v2_ref, bias_ref, o_ref, acc_ref):
    tq = q_ref.shape[2]
    zeros = jnp.zeros((HEAD_DIM, tq), bf16)
    k_refs = (kc_ref, k0_ref, k1_ref, k2_ref)
    v_refs = (vc_ref, v0_ref, v1_ref, v2_ref)

    def scores(h):
        pair = slice((h // 2) * LANES, (h // 2 + 1) * LANES)
        qh = q_ref[0, h * HEAD_DIM:(h + 1) * HEAD_DIM, :]
        qpad = jnp.concatenate([qh, zeros] if h % 2 == 0 else [zeros, qh], axis=0)
        s = [_dot(kr[0, :, pair], qpad) for kr in k_refs]
        for i in range(3):
            s[i + 1] = s[i + 1] + bias_ref[0, h, i * TM:(i + 1) * TM, :]
        return s

    s_next = scores(0)
    for h in range(NA_HEADS):
        s = s_next
        if h + 1 < NA_HEADS:
            s_next = scores(h + 1)
        m = functools.reduce(jnp.maximum, [jnp.max(x, axis=0, keepdims=True) for x in s])
        vrows = slice(h * V_EXT, (h + 1) * V_EXT)
        o = functools.reduce(jnp.add, [_dot(vr[0, 0, vrows, :], jnp.exp2(x - m).astype(bf16))
                                       for vr, x in zip(v_refs, s)])
        acc_ref[h * HEAD_DIM:(h + 1) * HEAD_DIM, :] = o[:HEAD_DIM] / o[HEAD_DIM:HEAD_DIM + 1]
    o_ref[0] = acc_ref[...].T.astype(bf16)


def _na(q_t, k, v_c, bias, off):
    B, _, Lc = q_t.shape
    T = Lc // TM
    nq = T - off
    nblk = T - 1

    def base(i):
        return 1 + jnp.clip(i + off - 2, 0, nblk - 3)

    def btype(i):
        qt = i + off
        return jnp.where(qt == 0, 3, jnp.where(qt == 1, 0, jnp.where(qt == T - 1, 2, 1)))

    kspec = lambda j: pl.BlockSpec((1, TM, 512), lambda b, i: (b, base(i) + j, 0))
    vrows = NA_HEADS * V_EXT
    vspec = lambda j: pl.BlockSpec((1, 1, vrows, TM), lambda b, i: (b, base(i) + j, 0, 0))
    return pl.pallas_call(
        _na_kernel,
        grid=(B, nq),
        in_specs=[
            pl.BlockSpec((1, 512, TM), lambda b, i: (b, 0, i + off)),
            pl.BlockSpec((1, TM, 512), lambda b, i: (b, 0, 0)), kspec(0), kspec(1), kspec(2),
            pl.BlockSpec((1, 1, vrows, TM), lambda b, i: (b, 0, 0, 0)), vspec(0), vspec(1), vspec(2),
            pl.BlockSpec((1, NA_HEADS, NA_KEYS, TM), lambda b, i: (btype(i), 0, 0, 0)),
        ],
        out_specs=pl.BlockSpec((1, TM, 512), lambda b, i: (b, i, 0)),
        out_shape=jax.ShapeDtypeStruct((B, nq * TM, 512), bf16),
        scratch_shapes=[pltpu.VMEM((512, TM), f32)],
        compiler_params=_cparams("parallel", "arbitrary"),
        name="na_attn",
    )(q_t, k, k, k, k, v_c, v_c, v_c, v_c, bias)


def _na_bias_table(rpb, rows):
    col = jnp.arange(GRID_W)
    dc = jnp.clip(col[:, None] - col[None, :], -(NA_WIN_COLS - 1), NA_WIN_COLS - 1) + (NA_WIN_COLS - 1)
    onehot = (dc[None] == jnp.arange(2 * NA_WIN_COLS - 1)[:, None, None]).astype(f32)
    tile = jnp.einsum("hrd,dkq->hrkq", rpb.astype(f32) * LOG2E, onehot, precision=lax.Precision.HIGHEST)
    c0 = jnp.clip(col - NA_WIN_COLS // 2, 0, GRID_W - NA_WIN_COLS)
    in_win = (col[:, None] >= c0[None, :]) & (col[:, None] < c0[None, :] + NA_WIN_COLS)
    tile = jnp.where(in_win, tile, NEG_INF)
    masked = jnp.full((rpb.shape[0], GRID_W, GRID_W), NEG_INF, f32)
    krows = NA_KEYS // GRID_W

    def one(r0, ks):
        out = []
        for kr in range(ks, ks + krows):
            parts = []
            for qr in range(r0, r0 + NA_QROWS):
                start = min(max(qr - NA_WIN_ROWS // 2, 0), rows - NA_WIN_ROWS)
                parts.append(tile[:, kr - qr + NA_WIN_ROWS - 1] if start <= kr < start + NA_WIN_ROWS else masked)
            out.append(jnp.concatenate(parts, axis=2))
        return jnp.concatenate(out, axis=1)

    tabs = [one(0, 0), one(NA_QROWS, 0), one(rows - NA_QROWS, rows - krows)]
    tabs.append(jnp.full_like(tabs[0], NEG_INF))
    return jnp.stack(tabs)


def _sw_kernel(q_ref, kc_ref, k0_ref, k1_ref, k2_ref, k3_ref, vc_ref, v0_ref, v1_ref, v2_ref, v3_ref,
               sink_ref, o_ref, acc_ref, *, off, ctx_len, total_len):
    tq = q_ref.shape[2]
    qt = pl.program_id(1) + off
    group = SW_Q_HEADS // SW_KV_HEADS
    k_refs = (k0_ref, k1_ref, k2_ref, k3_ref)
    v_refs = (v0_ref, v1_ref, v2_ref, v3_ref)
    kk = lax.broadcasted_iota(jnp.int32, (LANES, tq), 0)
    qpos = qt * tq + lax.broadcasted_iota(jnp.int32, (LANES, tq), 1)
    madd = []
    for i in range(4):
        kpos = (qt * (tq // LANES) - 1 + i) * LANES + kk
        ok = (kpos >= ctx_len) & (kpos < total_len) & (jnp.abs(kpos - qpos) <= SW_WINDOW) & (qt > 0)
        m1 = jnp.where(ok, 0.0, NEG_INF).astype(f32)
        madd.append(jnp.concatenate([m1] * group, axis=1))

    def scores(g):
        qg = jnp.concatenate([q_ref[0, (group * g + hh) * HEAD_DIM:(group * g + hh + 1) * HEAD_DIM, :]
                              for hh in range(group)], axis=1)
        zeros = jnp.zeros_like(qg)
        qpad = jnp.concatenate([qg, zeros] if g == 0 else [zeros, qg], axis=0)
        return [_dot(kc_ref[0], qpad)] + [_dot(kr[0], qpad) + madd[i] for i, kr in enumerate(k_refs)]

    s_next = scores(0)
    for g in range(SW_KV_HEADS):
        s = s_next
        if g + 1 < SW_KV_HEADS:
            s_next = scores(g + 1)
        kv_rows = slice(g * V_EXT, (g + 1) * V_EXT)
        sink = sink_ref[g]
        m = functools.reduce(jnp.maximum, [jnp.max(x, axis=0, keepdims=True) for x in s] + [sink])
        pc = jnp.exp2(s[0] - m).astype(bf16)
        parts = [_dot(vc_ref[0, j, kv_rows, :], pc[j * LANES:(j + 1) * LANES]) for j in range(tq // LANES)]
        parts += [_dot(vr[0, 0, kv_rows, :], jnp.exp2(x - m).astype(bf16)) for vr, x in zip(v_refs, s[1:])]
        o = functools.reduce(jnp.add, parts)
        o = o[:HEAD_DIM] / (o[HEAD_DIM:HEAD_DIM + 1] + jnp.exp2(sink - m))
        for hh in range(group):
            h = group * g + hh
            acc_ref[h * HEAD_DIM:(h + 1) * HEAD_DIM, :] = o[:, hh * tq:(hh + 1) * tq]
    o_ref[0] = acc_ref[...].T.astype(bf16)


def _sw(q_t, k, v_c, sink_rows, off, ctx_len):
    B, _, Lc = q_t.shape
    T = Lc // TM
    nq = T - off
    per = TM // LANES
    lo, hi = ctx_len // LANES, Lc // LANES - 1

    def blk(i, j):
        return jnp.clip((i + off) * per - 1 + j, lo, hi)

    kspec = lambda j: pl.BlockSpec((1, LANES, 128), lambda b, i: (b, blk(i, j), 0))
    vrows = SW_KV_HEADS * V_EXT
    vspec = lambda j: pl.BlockSpec((1, 1, vrows, LANES), lambda b, i: (b, blk(i, j), 0, 0))
    kern = functools.partial(_sw_kernel, off=off, ctx_len=ctx_len, total_len=Lc)
    return pl.pallas_call(
        kern,
        grid=(B, nq),
        in_specs=[
            pl.BlockSpec((1, 512, TM), lambda b, i: (b, 0, i + off)),
            pl.BlockSpec((1, ctx_len, 128), lambda b, i: (b, 0, 0)),
            kspec(0), kspec(1), kspec(2), kspec(3),
            pl.BlockSpec((1, ctx_len // LANES, vrows, LANES), lambda b, i: (b, 0, 0, 0)),
            vspec(0), vspec(1), vspec(2), vspec(3),
            pl.BlockSpec(sink_rows.shape, lambda b, i: (0, 0, 0)),
        ],
        out_specs=pl.BlockSpec((1, TM, 512), lambda b, i: (b, i, 0)),
        out_shape=jax.ShapeDtypeStruct((B, nq * TM, 512), bf16),
        scratch_shapes=[pltpu.VMEM((512, TM), f32)],
        compiler_params=_cparams("parallel", "arbitrary"),
        name="sw_attn",
    )(q_t, k, k, k, k, k, v_c, v_c, v_c, v_c, v_c, sink_rows)


def _mla_kernel(q_ref, k_ref, v_ref, o_ref, acc_ref, out_ref, *, off):
    nchunk = v_ref.shape[1]
    nstep = (nchunk - 1) // MLA_STEP_CHUNKS
    qt = pl.program_id(1) + off
    nk = jnp.where(qt == 0, 0, nstep)
    tq = q_ref.shape[2]
    acc_ref[...] = jnp.zeros_like(acc_ref)

    def step(ms, r0, chunks):
        nrows = len(chunks) * TM

        def scores(h):
            cols = slice(h * MLA_PAD, (h + 1) * MLA_PAD)
            return _dot(k_ref[0, pl.ds(r0, nrows), cols], q_ref[0, cols, :])

        ms_new = []
        s_next = scores(0)
        for h in range(MLA_HEADS):
            rows = slice(h * V_EXT, (h + 1) * V_EXT)
            s = s_next
            if h + 1 < MLA_HEADS:
                s_next = scores(h + 1)
            m_new = jnp.maximum(ms[h], jnp.max(s, axis=0, keepdims=True))
            alpha = jnp.exp2(ms[h] - m_new)
            p = jnp.exp2(s - m_new).astype(bf16)
            ms_new.append(m_new)
            upd = functools.reduce(jnp.add, [_dot(v_ref[0, c, rows, :], p[i * TM:(i + 1) * TM])
                                             for i, c in enumerate(chunks)])
            acc_ref[rows, :] = alpha * acc_ref[rows, :] + upd
        return tuple(ms_new)

    ms = step(tuple(jnp.full((1, tq), NEG_INF, f32) for _ in range(MLA_HEADS)), 0, [0])

    def body(i, ms):
        c0 = 1 + i * MLA_STEP_CHUNKS
        return step(ms, pl.multiple_of(c0 * TM, TM), [c0 + j for j in range(MLA_STEP_CHUNKS)])

    lax.fori_loop(0, nk, body, ms)
    for h in range(MLA_HEADS):
        num = acc_ref[h * V_EXT:h * V_EXT + MLA_V, :]
        out_ref[h * MLA_V:(h + 1) * MLA_V, :] = num / acc_ref[h * V_EXT + MLA_V:h * V_EXT + MLA_V + 1, :]
    o_ref[0] = out_ref[...].T.astype(bf16)


def _mla(q_t, k, v_c, off):
    B, _, Lc = q_t.shape
    T = Lc // TM
    nq = T - off
    return pl.pallas_call(
        functools.partial(_mla_kernel, off=off),
        grid=(B, nq),
        in_specs=[
            pl.BlockSpec((1, MLA_HEADS * MLA_PAD, TM), lambda b, i: (b, 0, i + off)),
            pl.BlockSpec((1, Lc, MLA_HEADS * MLA_PAD), lambda b, i: (b, 0, 0)),
            pl.BlockSpec((1, T, MLA_HEADS * V_EXT, TM), lambda b, i: (b, 0, 0, 0)),
        ],
        out_specs=pl.BlockSpec((1, TM, 512), lambda b, i: (b, i, 0)),
        out_shape=jax.ShapeDtypeStruct((B, nq * TM, 512), bf16),
        scratch_shapes=[pltpu.VMEM((MLA_HEADS * V_EXT, TM), f32), pltpu.VMEM((512, TM), f32)],
        compiler_params=_cparams("parallel", "arbitrary"),
        name="mla_attn",
    )(q_t, k, v_c)


def _merge_kernel(x_ref, a_ref, s_ref, ga_ref, ona_ref, osw_ref, oml_ref, wg_ref, wb_ref, wo_ref, o_ref):
    x = x_ref[0]
    d = x.shape[1]
    hb = _norm_mod(x, a_ref[0], s_ref[0]).astype(bf16)
    z = None
    for n, o_n in enumerate((ona_ref, osw_ref, oml_ref)):
        g = jax.nn.sigmoid(_dot(hb, wg_ref[:, n * d:(n + 1) * d]))
        y = g * _dot(o_n[0], wb_ref[n])
        z = y if z is None else z + y
    o_ref[0] = x + ga_ref[0] * _dot(z.astype(bf16), wo_ref[...])


def _merge(xc, a_row, s_row, g_row, o_na, o_sw, o_ml, w_gate, w_branch, w_out, off):
    B, Lc, D = xc.shape
    nt = Lc // TM - off
    nb = a_row.shape[0] - 1
    row = lambda b, i: (jnp.where(i + off == 0, nb, b), 0, 0)
    ospec = pl.BlockSpec((1, TM, BRANCH_W), lambda b, i: (b, i, 0))
    return pl.pallas_call(
        _merge_kernel,
        grid=(B, nt),
        in_specs=[
            pl.BlockSpec((1, TM, D), lambda b, i: (b, i + off, 0)),
            pl.BlockSpec((1, 1, D), row), pl.BlockSpec((1, 1, D), row), pl.BlockSpec((1, 1, D), row),
            ospec, ospec, ospec,
            _const_spec(w_gate.shape), _const_spec(w_branch.shape), _const_spec(w_out.shape),
        ],
        out_specs=pl.BlockSpec((1, TM, D), lambda b, i: (b, i, 0)),
        out_shape=jax.ShapeDtypeStruct((B, nt * TM, D), f32),
        compiler_params=_cparams("parallel", "parallel"),
        name="merge",
    )(xc, a_row, s_row, g_row, o_na, o_sw, o_ml, w_gate, w_branch, w_out)


def _ffn_kernel(x_ref, xp_ref, xn_ref, a_ref, s_ref, gf_ref, wup_ref, cw_ref, cb_ref, wdn_ref, o_ref, acc_ref,
                *, ctx_tiles):
    tm = x_ref.shape[1]
    halo = xp_ref.shape[1]
    t = pl.program_id(1)
    nt = pl.num_programs(1)
    x = x_ref[0]
    xe = jnp.concatenate([xp_ref[0], x, xn_ref[0]], axis=0)
    hb = _norm_mod(xe, a_ref[0], s_ref[0]).astype(bf16)
    left_ok = jnp.where(jnp.logical_and(t != 0, t != ctx_tiles), 1.0, 0.0)
    right_ok = jnp.where(jnp.logical_and(t != nt - 1, t != ctx_tiles - 1), 1.0, 0.0)
    ri = lax.broadcasted_iota(jnp.int32, (tm + 2 * halo, 1), 0)
    keep = jnp.where(ri < halo, left_ok, jnp.where(ri >= tm + halo, right_ok, 1.0)).astype(f32)
    acc_ref[...] = jnp.zeros_like(acc_ref)

    def chunk(c, carry):
        u = _dot(hb, wup_ref[c]) * keep
        w = cw_ref[c]
        uc = (w[0:1] * u[halo - 1:halo - 1 + tm] + w[1:2] * u[halo:halo + tm]
              + w[2:3] * u[halo + 1:halo + 1 + tm] + cb_ref[c])
        g, v = uc[:, :FF_CHUNK], uc[:, FF_CHUNK:]
        act = (g * jax.nn.sigmoid(g) * v).astype(bf16)
        acc_ref[...] += _dot(act, wdn_ref[c])
        return carry

    lax.fori_loop(0, wup_ref.shape[0], chunk, 0)
    o_ref[0] = x + gf_ref[0] * acc_ref[...]


def _ffn(x1, a_row, s_row, g_row, w_up_c, conv_w_c, conv_b_c, w_down_c, ctx_tiles):
    B, Lt, D = x1.shape
    nt = Lt // TM
    nb = a_row.shape[0] - 1
    per = TM // SUBLANES
    last = Lt // SUBLANES - 1
    row = lambda b, t: (jnp.where(t < ctx_tiles, nb, b), 0, 0)
    return pl.pallas_call(
        functools.partial(_ffn_kernel, ctx_tiles=ctx_tiles),
        grid=(B, nt),
        in_specs=[
            pl.BlockSpec((1, TM, D), lambda b, t: (b, t, 0)),
            pl.BlockSpec((1, SUBLANES, D), lambda b, t: (b, jnp.maximum(t * per - 1, 0), 0)),
            pl.BlockSpec((1, SUBLANES, D), lambda b, t: (b, jnp.minimum((t + 1) * per, last), 0)),
            pl.BlockSpec((1, 1, D), row), pl.BlockSpec((1, 1, D), row), pl.BlockSpec((1, 1, D), row),
            _const_spec(w_up_c.shape), _const_spec(conv_w_c.shape), _const_spec(conv_b_c.shape),
            _const_spec(w_down_c.shape),
        ],
        out_specs=pl.BlockSpec((1, TM, D), lambda b, t: (b, t, 0)),
        out_shape=jax.ShapeDtypeStruct((B, Lt, D), f32),
        scratch_shapes=[pltpu.VMEM((TM, D), f32)],
        compiler_params=_cparams("parallel", "arbitrary"),
        name="ffn",
    )(x1, x1, x1, a_row, s_row, g_row, w_up_c, conv_w_c, conv_b_c, w_down_c)


def _rope_table(seq, ctx_len, n):
    t = jnp.arange(seq)
    inv = ROPE_BASE ** (-jnp.arange(n, dtype=f32) / n)
    ar = (t // GRID_W).astype(f32)[None, :] * inv[:, None]
    ac = (t % GRID_W).astype(f32)[None, :] * inv[:, None]
    lat = jnp.concatenate([jnp.cos(ar), jnp.sin(ar), jnp.cos(ac), jnp.sin(ac)], axis=0)
    one, zero = jnp.ones((n, ctx_len), f32), jnp.zeros((n, ctx_len), f32)
    return jnp.concatenate([jnp.concatenate([one, zero, one, zero], axis=0), lat], axis=1)


def _chunked_ffn_weights(w_up, conv_w, conv_b, w_down):
    d_ff = w_down.shape[0]
    nc = d_ff // FF_CHUNK
    pair = lambda a: jnp.concatenate([a[..., :d_ff].reshape(a.shape[:-1] + (nc, FF_CHUNK)),
                                      a[..., d_ff:].reshape(a.shape[:-1] + (nc, FF_CHUNK))], axis=-1)
    w_up_c = jnp.moveaxis(pair(w_up), 1, 0).astype(bf16)
    conv_w_c = jnp.moveaxis(pair(conv_w), 1, 0)
    conv_b_c = pair(conv_b)[:, None, :]
    w_down_c = w_down.reshape(nc, FF_CHUNK, w_down.shape[1]).astype(bf16)
    return w_up_c, conv_w_c, conv_b_c, w_down_c


def kernel(x, c, ctx, c_ctx, w_ada, b_ada, g_mix, g_ffn, w_in, na_q_norm, na_k_norm, na_rpb, sw_q_norm, sw_k_norm,
           sw_sink, mla_q_rank_norm, mla_kv_rank_norm, w_uq, w_ukv, mla_q_norm, mla_k_norm, w_branch, w_out,
           w_up, conv_w, conv_b, w_down):
    B, S, D = x.shape
    C = ctx.shape[1]
    depth = w_ada.shape[0]
    assert C == TM and S % TM == 0 and GRID_W * NA_QROWS == TM and (S // GRID_W) * GRID_W == S
    assert S // GRID_W >= NA_KEYS // GRID_W and w_down.shape[1] % FF_CHUNK == 0
    assert (S // TM) % MLA_STEP_CHUNKS == 0
    rows = S // GRID_W

    xc = jnp.concatenate([ctx, x], axis=1)
    n_mod = -(-(B + 1) // SUBLANES) * SUBLANES
    c_all = jnp.zeros((n_mod, D), f32).at[:B].set(c).at[B].set(c_ctx)
    zero = jnp.zeros_like(g_mix)
    g_sel = jnp.stack([zero, g_mix, zero, zero, g_ffn, zero], axis=1)[:, :, None, :]
    mod = _ada(c_all, w_ada, b_ada, g_sel)

    rope_sw = _rope_table(S, C, HEAD_DIM // 4)
    rope_ml = _rope_table(S, C, MLA_ROPE // 4)
    group = SW_Q_HEADS // SW_KV_HEADS

    for l in range(depth):
        off = 0 if l < depth - 1 else 1
        m6 = mod[l, :B + 1].reshape(B + 1, 6, 1, D)
        shift_a, a_mix, gate_a, shift_f, a_ffn, gate_f = (m6[:, j] for j in range(6))
        w_in_t = w_in[l][:, :N_STREAM].T.astype(bf16)
        gains = jnp.concatenate([na_q_norm[l], na_k_norm[l], sw_q_norm[l], sw_k_norm[l], mla_q_rank_norm[l],
                                 mla_kv_rank_norm[l], mla_q_norm[l], mla_k_norm[l]])[:, None]
        (na_q, na_k, na_v, sw_q, sw_k, sw_v, m_q, m_k, m_v) = _proj(
            xc, a_mix, shift_a, w_in_t, w_uq[l].T.astype(bf16), w_ukv[l].T.astype(bf16),
            gains, rope_sw, rope_ml)
        o_na = _na(na_q, na_k, na_v, _na_bias_table(na_rpb[l], rows), off)
        sink_rows = jnp.repeat(sw_sink[l].reshape(SW_KV_HEADS, 1, group) * LOG2E, TM, axis=2)
        o_sw = _sw(sw_q, sw_k, sw_v, sink_rows, off, C)
        o_ml = _mla(m_q, m_k, m_v, off)
        x1 = _merge(xc, a_mix, shift_a, gate_a, o_na, o_sw, o_ml, w_in[l][:, N_STREAM:].astype(bf16),
                    w_branch[l].astype(bf16), w_out[l].astype(bf16), off)
        xc = _ffn(x1, a_ffn, shift_f, gate_f, *_chunked_ffn_weights(w_up[l], conv_w[l], conv_b[l], w_down[l]),
                  ctx_tiles=1 - off)
    return xc
```

```python
import functools

import jax
import jax.numpy as jnp
from jax import lax
from jax.experimental import pallas as pl
from jax.experimental.pallas import tpu as pltpu

f32 = jnp.float32
bf16 = jnp.bfloat16

GRID_W = 64
HEAD_DIM = 64
ROPE_BASE = 10000.0
NORM_EPS = 1e-6
NEG_INF = -1e30
NA_HEADS = 8
NA_WIN_ROWS = 8
NA_WIN_COLS = 16
SW_Q_HEADS = 8
SW_KV_HEADS = 2
SW_WINDOW = 128
MLA_HEADS = 8
MLA_Q_RANK = 384
MLA_KV_RANK = 256
MLA_NOPE = 64
MLA_ROPE = 32
MLA_V = 64
N_BRANCH = 3
BRANCH_W = 512
CONV_W = 3

LANES = 128
SUBLANES = 8
TM = 256
MLA_PAD = 128
MLA_STEP_CHUNKS = 4
QK_AHEAD = 3
PV_BEHIND = 2
FF_CHUNK = 256
NA_QROWS = TM // GRID_W
NA_KEYS = 3 * TM
BF16_ROWS = 16
V_EXT = HEAD_DIM + BF16_ROWS
LOG2E = 1.4426950408889634
VMEM_LIMIT = 56 * 1024 * 1024

_SEG = {}
_o = 0
for _name, _n in (("na_q", 512), ("na_k", 512), ("na_v", 512), ("sw_q", 512), ("sw_k", 128), ("sw_v", 128),
                  ("c_q", MLA_Q_RANK), ("c_kv", MLA_KV_RANK), ("k_r", MLA_ROPE)):
    _SEG[_name] = (_o, _o + _n)
    _o += _n
N_STREAM = _o
_GN = {}
_o = 0
for _name, _n in (("na_q", 64), ("na_k", 64), ("sw_q", 64), ("sw_k", 64), ("mla_qr", MLA_Q_RANK),
                  ("mla_kvr", MLA_KV_RANK), ("mla_q", 96), ("mla_k", 96)):
    _GN[_name] = (_o, _o + _n)
    _o += _n
N_GAIN = _o


def _dot(a, b):
    return jnp.dot(a, b, preferred_element_type=f32)


def _cparams(*sem):
    return pltpu.CompilerParams(dimension_semantics=sem, vmem_limit_bytes=VMEM_LIMIT)


def _const_spec(shape):
    nd = len(shape)
    return pl.BlockSpec(shape, lambda *_: (0,) * nd, pipeline_mode=pl.Buffered(1))


def _ada_kernel(c_ref, w_ref, b_ref, g_ref, o_ref):
    n = pl.program_id(1)
    c = c_ref[...]
    sc = (c * jax.nn.sigmoid(c)).astype(bf16)
    mod = _dot(sc, w_ref[0].astype(bf16)) + b_ref[0]
    fold = jnp.logical_or(n == 1, n == 4)
    o_ref[0] = jnp.where(fold, g_ref[0, 0] * (1.0 + mod), mod)


def _ada(c_all, w_ada, b_ada, g_sel):
    L, D, _ = w_ada.shape
    R = c_all.shape[0]
    return pl.pallas_call(
        _ada_kernel,
        grid=(L, 6),
        in_specs=[
            pl.BlockSpec((R, D), lambda l, n: (0, 0)),
            pl.BlockSpec((1, D, D), lambda l, n: (l, 0, n)),
            pl.BlockSpec((1, 1, D), lambda l, n: (l, 0, n)),
            pl.BlockSpec((1, 1, 1, D), lambda l, n: (l, n, 0, 0)),
        ],
        out_specs=pl.BlockSpec((1, R, D), lambda l, n: (l, 0, n)),
        out_shape=jax.ShapeDtypeStruct((L, R, 6 * D), f32),
        compiler_params=_cparams("parallel", "parallel"),
        name="ada",
    )(c_all, w_ada, b_ada.reshape(L, 1, 6 * D), g_sel)


def _norm_mod(x, a, s):
    ms = jnp.mean(x * x, axis=-1, keepdims=True)
    return (x * lax.rsqrt(ms + NORM_EPS)) * a + s


def _head_norm(y3, gain):
    ms = jnp.mean(y3 * y3, axis=1, keepdims=True)
    return y3 * lax.rsqrt(ms + NORM_EPS) * gain[None]


def _rope(t3, tab, lo, n):
    cr, sr, cc, sc = (tab[i * n:(i + 1) * n][None] for i in range(4))
    a, b, c, e = (t3[:, lo + i * n:lo + (i + 1) * n] for i in range(4))
    return [a * cr - b * sr, a * sr + b * cr, c * cc - e * sc, c * sc + e * cc]


def _pipelined_heads(n, scores, softmax, weighted_values):
    s, pending = {}, []
    for h in range(min(QK_AHEAD, n)):
        s[h] = scores(h)
    for h in range(n):
        if h + QK_AHEAD < n:
            s[h + QK_AHEAD] = scores(h + QK_AHEAD)
        pending.append((h, softmax(h, s.pop(h))))
        if len(pending) > PV_BEHIND:
            weighted_values(*pending.pop(0))
    for item in pending:
        weighted_values(*item)


def _proj_kernel(x_ref, a_ref, s_ref, win_ref, wuq_ref, wukv_ref, gn_ref, rsw_ref, rml_ref,
                 naq_ref, nak_ref, nav_ref, swq_ref, swk_ref, swv_ref, mq_ref, mk_ref, mv_ref):
    tm = x_ref.shape[1]
    h = _norm_mod(x_ref[0], a_ref[0], s_ref[0])
    ht = h.T.astype(bf16)

    def seg(name):
        lo, hi = _SEG[name]
        return _dot(win_ref[lo:hi, :], ht)

    def gain(name):
        lo, hi = _GN[name]
        return gn_ref[lo:hi, :]

    def with_ones(v, heads):
        v3 = v.reshape(heads, HEAD_DIM, tm)
        ones = jnp.ones((heads, BF16_ROWS, tm), f32)
        return jnp.concatenate([v3, ones], axis=1).reshape(heads * V_EXT, tm).astype(bf16)

    qs = HEAD_DIM ** -0.5 * LOG2E
    q = _head_norm(seg("na_q").reshape(NA_HEADS, HEAD_DIM, tm), gain("na_q")) * qs
    naq_ref[0] = q.reshape(NA_HEADS * HEAD_DIM, tm).astype(bf16)
    k = _head_norm(seg("na_k").reshape(NA_HEADS, HEAD_DIM, tm), gain("na_k"))
    nak_ref[0] = k.reshape(NA_HEADS * HEAD_DIM, tm).T.astype(bf16)
    nav_ref[0, 0] = with_ones(seg("na_v"), NA_HEADS)

    rsw = rsw_ref[...]
    nq = HEAD_DIM // 4
    q = _head_norm(seg("sw_q").reshape(SW_Q_HEADS, HEAD_DIM, tm), gain("sw_q"))
    q = jnp.concatenate(_rope(q, rsw, 0, nq), axis=1) * qs
    swq_ref[0] = q.reshape(SW_Q_HEADS * HEAD_DIM, tm).astype(bf16)
    k = _head_norm(seg("sw_k").reshape(SW_KV_HEADS, HEAD_DIM, tm), gain("sw_k"))
    k = jnp.concatenate(_rope(k, rsw, 0, nq), axis=1)
    swk_ref[0] = k.reshape(SW_KV_HEADS * HEAD_DIM, tm).T.astype(bf16)
    v = with_ones(seg("sw_v"), SW_KV_HEADS)
    for j in range(tm // LANES):
        swv_ref[0, j] = v[:, j * LANES:(j + 1) * LANES]

    rml = rml_ref[...]
    nr = MLA_ROPE // 4
    dq = MLA_NOPE + MLA_ROPE
    pad = jnp.zeros((MLA_HEADS, MLA_PAD - dq, tm), f32)
    cq = seg("c_q")
    cq = cq * lax.rsqrt(jnp.mean(cq * cq, axis=0, keepdims=True) + NORM_EPS) * gain("mla_qr")
    mq = _head_norm(_dot(wuq_ref[...], cq.astype(bf16)).reshape(MLA_HEADS, dq, tm), gain("mla_q"))
    mq = jnp.concatenate([mq[:, :MLA_NOPE]] + _rope(mq, rml, MLA_NOPE, nr), axis=1) * (dq ** -0.5 * LOG2E)
    mq_ref[0] = jnp.concatenate([mq, pad], axis=1).reshape(MLA_HEADS * MLA_PAD, tm).astype(bf16)

    ckv = seg("c_kv")
    ckv = ckv * lax.rsqrt(jnp.mean(ckv * ckv, axis=0, keepdims=True) + NORM_EPS) * gain("mla_kvr")
    kv = _dot(wukv_ref[...], ckv.astype(bf16)).reshape(MLA_HEADS, MLA_NOPE + MLA_V, tm)
    kn, v = kv[:, :MLA_NOPE], kv[:, MLA_NOPE:]
    kr = seg("k_r")
    ss = jnp.sum(kn * kn, axis=1, keepdims=True) + jnp.sum(kr * kr, axis=0, keepdims=True)[None]
    r = lax.rsqrt(ss / dq + NORM_EPS)
    gk = gain("mla_k")
    kn = kn * r * gk[:MLA_NOPE][None]
    krh = kr[None] * r * gk[MLA_NOPE:][None]
    mk = jnp.concatenate([kn] + _rope(krh, rml, 0, nr) + [pad], axis=1)
    mk_ref[0] = mk.reshape(MLA_HEADS * MLA_PAD, tm).T.astype(bf16)
    mv_ref[0, 0] = with_ones(v.reshape(MLA_HEADS * MLA_V, tm), MLA_HEADS)


def _proj(xc, a_row, s_row, win_t, wuq_t, wukv_t, gains, rope_sw, rope_ml):
    B, Lc, D = xc.shape
    T = Lc // TM
    nb = a_row.shape[0] - 1
    row = lambda b, t: (jnp.where(t == 0, nb, b), 0, 0)
    fm = lambda n: pl.BlockSpec((1, n, TM), lambda b, t: (b, 0, t))
    tk = lambda n: pl.BlockSpec((1, TM, n), lambda b, t: (b, t, 0))
    sds = jax.ShapeDtypeStruct
    return pl.pallas_call(
        _proj_kernel,
        grid=(B, T),
        in_specs=[
            pl.BlockSpec((1, TM, D), lambda b, t: (b, t, 0)),
            pl.BlockSpec((1, 1, D), row),
            pl.BlockSpec((1, 1, D), row),
            _const_spec(win_t.shape), _const_spec(wuq_t.shape), _const_spec(wukv_t.shape),
            _const_spec(gains.shape),
            pl.BlockSpec((rope_sw.shape[0], TM), lambda b, t: (0, t)),
            pl.BlockSpec((rope_ml.shape[0], TM), lambda b, t: (0, t)),
        ],
        out_specs=[
            fm(512), tk(512), pl.BlockSpec((1, 1, NA_HEADS * V_EXT, TM), lambda b, t: (b, t, 0, 0)),
            fm(512), tk(128),
            pl.BlockSpec((1, TM // LANES, SW_KV_HEADS * V_EXT, LANES), lambda b, t: (b, t, 0, 0)),
            fm(MLA_HEADS * MLA_PAD), tk(MLA_HEADS * MLA_PAD),
            pl.BlockSpec((1, 1, MLA_HEADS * V_EXT, TM), lambda b, t: (b, t, 0, 0)),
        ],
        out_shape=[
            sds((B, 512, Lc), bf16), sds((B, Lc, 512), bf16), sds((B, T, NA_HEADS * V_EXT, TM), bf16),
            sds((B, 512, Lc), bf16), sds((B, Lc, 128), bf16),
            sds((B, Lc // LANES, SW_KV_HEADS * V_EXT, LANES), bf16),
            sds((B, MLA_HEADS * MLA_PAD, Lc), bf16), sds((B, Lc, MLA_HEADS * MLA_PAD), bf16),
            sds((B, T, MLA_HEADS * V_EXT, TM), bf16),
        ],
        compiler_params=_cparams("parallel", "parallel"),
        name="proj",
    )(xc, a_row, s_row, win_t, wuq_t, wukv_t, gains, rope_sw, rope_ml)


def _na_kernel(q_ref, kc_ref, k0_ref, k1_ref, k2_ref, vc_ref, v0_ref, v1_ref, v2_ref, bias_ref, o_ref, acc_ref):
    tq = q_ref.shape[2]
    zeros = jnp.zeros((HEAD_DIM, tq), bf16)
    k_refs = (kc_ref, k0_ref, k1_ref, k2_ref)
    v_refs = (vc_ref, v0_ref, v1_ref, v2_ref)

    def scores(h):
        pair = slice((h // 2) * LANES, (h // 2 + 1) * LANES)
        qh = q_ref[0, h * HEAD_DIM:(h + 1) * HEAD_DIM, :]
        qpad = jnp.concatenate([qh, zeros] if h % 2 == 0 else [zeros, qh], axis=0)
        s = [_dot(kr[0, :, pair], qpad) for kr in k_refs]
        for i in range(3):
            s[i + 1] = s[i + 1] + bias_ref[0, h, i * TM:(i + 1) * TM, :]
        return s

    def softmax(h, s):
        m = functools.reduce(jnp.maximum, [jnp.max(x, axis=0, keepdims=True) for x in s])
        return [jnp.exp2(x - m).astype(bf16) for x in s]

    def weighted_values(h, p):
        vrows = slice(h * V_EXT, (h + 1) * V_EXT)
        o = functools.reduce(jnp.add, [_dot(vr[0, 0, vrows, :], x) for vr, x in zip(v_refs, p)])
        acc_ref[h * HEAD_DIM:(h + 1) * HEAD_DIM, :] = o[:HEAD_DIM] / o[HEAD_DIM:HEAD_DIM + 1]

    _pipelined_heads(NA_HEADS, scores, softmax, weighted_values)
    o_ref[0] = acc_ref[...].T.astype(bf16)


def _na(q_t, k, v_c, bias, off):
    B, _, Lc = q_t.shape
    T = Lc // TM
    nq = T - off
    nblk = T - 1

    def base(i):
        return 1 + jnp.clip(i + off - 2, 0, nblk - 3)

    def btype(i):
        qt = i + off
        return jnp.where(qt == 0, 3, jnp.where(qt == 1, 0, jnp.where(qt == T - 1, 2, 1)))

    kspec = lambda j: pl.BlockSpec((1, TM, 512), lambda b, i: (b, base(i) + j, 0))
    vrows = NA_HEADS * V_EXT
    vspec = lambda j: pl.BlockSpec((1, 1, vrows, TM), lambda b, i: (b, base(i) + j, 0, 0))
    return pl.pallas_call(
        _na_kernel,
        grid=(B, nq),
        in_specs=[
            pl.BlockSpec((1, 512, TM), lambda b, i: (b, 0, i + off)),
            pl.BlockSpec((1, TM, 512), lambda b, i: (b, 0, 0)), kspec(0), kspec(1), kspec(2),
            pl.BlockSpec((1, 1, vrows, TM), lambda b, i: (b, 0, 0, 0)), vspec(0), vspec(1), vspec(2),
            pl.BlockSpec((1, NA_HEADS, NA_KEYS, TM), lambda b, i: (btype(i), 0, 0, 0)),
        ],
        out_specs=pl.BlockSpec((1, TM, 512), lambda b, i: (b, i, 0)),
        out_shape=jax.ShapeDtypeStruct((B, nq * TM, 512), bf16),
        scratch_shapes=[pltpu.VMEM((512, TM), f32)],
        compiler_params=_cparams("parallel", "arbitrary"),
        name="na_attn",
    )(q_t, k, k, k, k, v_c, v_c, v_c, v_c, bias)


def _na_bias_table(rpb, rows):
    col = jnp.arange(GRID_W)
    dc = jnp.clip(col[:, None] - col[None, :], -(NA_WIN_COLS - 1), NA_WIN_COLS - 1) + (NA_WIN_COLS - 1)
    onehot = (dc[None] == jnp.arange(2 * NA_WIN_COLS - 1)[:, None, None]).astype(f32)
    tile = jnp.einsum("hrd,dkq->hrkq", rpb.astype(f32) * LOG2E, onehot, precision=lax.Precision.HIGHEST)
    c0 = jnp.clip(col - NA_WIN_COLS // 2, 0, GRID_W - NA_WIN_COLS)
    in_win = (col[:, None] >= c0[None, :]) & (col[:, None] < c0[None, :] + NA_WIN_COLS)
    tile = jnp.where(in_win, tile, NEG_INF)
    masked = jnp.full((rpb.shape[0], GRID_W, GRID_W), NEG_INF, f32)
    krows = NA_KEYS // GRID_W

    def one(r0, ks):
        out = []
        for kr in range(ks, ks + krows):
            parts = []
            for qr in range(r0, r0 + NA_QROWS):
                start = min(max(qr - NA_WIN_ROWS // 2, 0), rows - NA_WIN_ROWS)
                parts.append(tile[:, kr - qr + NA_WIN_ROWS - 1] if start <= kr < start + NA_WIN_ROWS else masked)
            out.append(jnp.concatenate(parts, axis=2))
        return jnp.concatenate(out, axis=1)

    tabs = [one(0, 0), one(NA_QROWS, 0), one(rows - NA_QROWS, rows - krows)]
    tabs.append(jnp.full_like(tabs[0], NEG_INF))
    return jnp.stack(tabs)


def _sw_kernel(q_ref, kc_ref, k0_ref, k1_ref, k2_ref, k3_ref, vc_ref, v0_ref, v1_ref, v2_ref, v3_ref,
               sink_ref, o_ref, acc_ref, *, off, ctx_len, total_len):
    tq = q_ref.shape[2]
    qt = pl.program_id(1) + off
    group = SW_Q_HEADS // SW_KV_HEADS
    k_refs = (k0_ref, k1_ref, k2_ref, k3_ref)
    v_refs = (v0_ref, v1_ref, v2_ref, v3_ref)
    kk = lax.broadcasted_iota(jnp.int32, (LANES, tq), 0)
    qpos = qt * tq + lax.broadcasted_iota(jnp.int32, (LANES, tq), 1)
    madd = []
    for i in range(4):
        kpos = (qt * (tq // LANES) - 1 + i) * LANES + kk
        ok = (kpos >= ctx_len) & (kpos < total_len) & (jnp.abs(kpos - qpos) <= SW_WINDOW) & (qt > 0)
        madd.append(jnp.where(ok, 0.0, NEG_INF).astype(f32))
    zeros = jnp.zeros((HEAD_DIM, tq), bf16)

    def scores(h):
        qh = q_ref[0, h * HEAD_DIM:(h + 1) * HEAD_DIM, :]
        qpad = jnp.concatenate([qh, zeros] if h // group == 0 else [zeros, qh], axis=0)
        return [_dot(kc_ref[0], qpad)] + [_dot(kr[0], qpad) + madd[i] for i, kr in enumerate(k_refs)]

    def softmax(h, s):
        sink = sink_ref[h]
        m = functools.reduce(jnp.maximum, [jnp.max(x, axis=0, keepdims=True) for x in s] + [sink])
        return [jnp.exp2(x - m).astype(bf16) for x in s], jnp.exp2(sink - m)

    def weighted_values(h, item):
        p, p_sink = item
        kv_rows = slice((h // group) * V_EXT, (h // group + 1) * V_EXT)
        parts = [_dot(vc_ref[0, j, kv_rows, :], p[0][j * LANES:(j + 1) * LANES]) for j in range(tq // LANES)]
        parts += [_dot(vr[0, 0, kv_rows, :], x) for vr, x in zip(v_refs, p[1:])]
        o = functools.reduce(jnp.add, parts)
        acc_ref[h * HEAD_DIM:(h + 1) * HEAD_DIM, :] = o[:HEAD_DIM] / (o[HEAD_DIM:HEAD_DIM + 1] + p_sink)

    _pipelined_heads(SW_Q_HEADS, scores, softmax, weighted_values)
    o_ref[0] = acc_ref[...].T.astype(bf16)


def _sw(q_t, k, v_c, sink_rows, off, ctx_len):
    B, _, Lc = q_t.shape
    T = Lc // TM
    nq = T - off
    per = TM // LANES
    lo, hi = ctx_len // LANES, Lc // LANES - 1

    def blk(i, j):
        return jnp.clip((i + off) * per - 1 + j, lo, hi)

    kspec = lambda j: pl.BlockSpec((1, LANES, 128), lambda b, i: (b, blk(i, j), 0))
    vrows = SW_KV_HEADS * V_EXT
    vspec = lambda j: pl.BlockSpec((1, 1, vrows, LANES), lambda b, i: (b, blk(i, j), 0, 0))
    kern = functools.partial(_sw_kernel, off=off, ctx_len=ctx_len, total_len=Lc)
    return pl.pallas_call(
        kern,
        grid=(B, nq),
        in_specs=[
            pl.BlockSpec((1, 512, TM), lambda b, i: (b, 0, i + off)),
            pl.BlockSpec((1, ctx_len, 128), lambda b, i: (b, 0, 0)),
            kspec(0), kspec(1), kspec(2), kspec(3),
            pl.BlockSpec((1, ctx_len // LANES, vrows, LANES), lambda b, i: (b, 0, 0, 0)),
            vspec(0), vspec(1), vspec(2), vspec(3),
            pl.BlockSpec(sink_rows.shape, lambda b, i: (0, 0, 0)),
        ],
        out_specs=pl.BlockSpec((1, TM, 512), lambda b, i: (b, i, 0)),
        out_shape=jax.ShapeDtypeStruct((B, nq * TM, 512), bf16),
        scratch_shapes=[pltpu.VMEM((512, TM), f32)],
        compiler_params=_cparams("parallel", "arbitrary"),
        name="sw_attn",
    )(q_t, k, k, k, k, k, v_c, v_c, v_c, v_c, v_c, sink_rows)


def _mla_kernel(q_ref, k_ref, v_ref, o_ref, acc_ref, out_ref, *, off):
    nchunk = v_ref.shape[1]
    nstep = (nchunk - 1) // MLA_STEP_CHUNKS
    qt = pl.program_id(1) + off
    nk = jnp.where(qt == 0, 0, nstep)
    tq = q_ref.shape[2]
    acc_ref[...] = jnp.zeros_like(acc_ref)

    def step(ms, r0, chunks):
        nrows = len(chunks) * TM

        def scores(h):
            cols = slice(h * MLA_PAD, (h + 1) * MLA_PAD)
            return _dot(k_ref[0, pl.ds(r0, nrows), cols], q_ref[0, cols, :])

        ms_new = [None] * MLA_HEADS

        def softmax(h, s):
            ms_new[h] = jnp.maximum(ms[h], jnp.max(s, axis=0, keepdims=True))
            return jnp.exp2(ms[h] - ms_new[h]), jnp.exp2(s - ms_new[h]).astype(bf16)

        def weighted_values(h, item):
            alpha, p = item
            rows = slice(h * V_EXT, (h + 1) * V_EXT)
            upd = functools.reduce(jnp.add, [_dot(v_ref[0, c, rows, :], p[i * TM:(i + 1) * TM])
                                             for i, c in enumerate(chunks)])
            acc_ref[rows, :] = alpha * acc_ref[rows, :] + upd

        _pipelined_heads(MLA_HEADS, scores, softmax, weighted_values)
        return tuple(ms_new)

    ms = step(tuple(jnp.full((1, tq), NEG_INF, f32) for _ in range(MLA_HEADS)), 0, [0])

    def body(i, ms):
        c0 = 1 + i * MLA_STEP_CHUNKS
        return step(ms, pl.multiple_of(c0 * TM, TM), [c0 + j for j in range(MLA_STEP_CHUNKS)])

    lax.fori_loop(0, nk, body, ms)
    for h in range(MLA_HEADS):
        num = acc_ref[h * V_EXT:h * V_EXT + MLA_V, :]
        out_ref[h * MLA_V:(h + 1) * MLA_V, :] = num / acc_ref[h * V_EXT + MLA_V:h * V_EXT + MLA_V + 1, :]
    o_ref[0] = out_ref[...].T.astype(bf16)


def _mla(q_t, k, v_c, off):
    B, _, Lc = q_t.shape
    T = Lc // TM
    nq = T - off
    return pl.pallas_call(
        functools.partial(_mla_kernel, off=off),
        grid=(B, nq),
        in_specs=[
            pl.BlockSpec((1, MLA_HEADS * MLA_PAD, TM), lambda b, i: (b, 0, i + off)),
            pl.BlockSpec((1, Lc, MLA_HEADS * MLA_PAD), lambda b, i: (b, 0, 0)),
            pl.BlockSpec((1, T, MLA_HEADS * V_EXT, TM), lambda b, i: (b, 0, 0, 0)),
        ],
        out_specs=pl.BlockSpec((1, TM, 512), lambda b, i: (b, i, 0)),
        out_shape=jax.ShapeDtypeStruct((B, nq * TM, 512), bf16),
        scratch_shapes=[pltpu.VMEM((MLA_HEADS * V_EXT, TM), f32), pltpu.VMEM((512, TM), f32)],
        compiler_params=_cparams("parallel", "arbitrary"),
        name="mla_attn",
    )(q_t, k, v_c)


def _merge_kernel(x_ref, a_ref, s_ref, ga_ref, ona_ref, osw_ref, oml_ref, wg_ref, wb_ref, wo_ref, o_ref):
    x = x_ref[0]
    d = x.shape[1]
    hb = _norm_mod(x, a_ref[0], s_ref[0]).astype(bf16)
    z = None
    for n, o_n in enumerate((ona_ref, osw_ref, oml_ref)):
        g = jax.nn.sigmoid(_dot(hb, wg_ref[:, n * d:(n + 1) * d]))
        y = g * _dot(o_n[0], wb_ref[n])
        z = y if z is None else z + y
    o_ref[0] = x + ga_ref[0] * _dot(z.astype(bf16), wo_ref[...])


def _merge(xc, a_row, s_row, g_row, o_na, o_sw, o_ml, w_gate, w_branch, w_out, off):
    B, Lc, D = xc.shape
    nt = Lc // TM - off
    nb = a_row.shape[0] - 1
    row = lambda b, i: (jnp.where(i + off == 0, nb, b), 0, 0)
    ospec = pl.BlockSpec((1, TM, BRANCH_W), lambda b, i: (b, i, 0))
    return pl.pallas_call(
        _merge_kernel,
        grid=(B, nt),
        in_specs=[
            pl.BlockSpec((1, TM, D), lambda b, i: (b, i + off, 0)),
            pl.BlockSpec((1, 1, D), row), pl.BlockSpec((1, 1, D), row), pl.BlockSpec((1, 1, D), row),
            ospec, ospec, ospec,
            _const_spec(w_gate.shape), _const_spec(w_branch.shape), _const_spec(w_out.shape),
        ],
        out_specs=pl.BlockSpec((1, TM, D), lambda b, i: (b, i, 0)),
        out_shape=jax.ShapeDtypeStruct((B, nt * TM, D), f32),
        compiler_params=_cparams("parallel", "parallel"),
        name="merge",
    )(xc, a_row, s_row, g_row, o_na, o_sw, o_ml, w_gate, w_branch, w_out)


def _ffn_kernel(x_ref, xp_ref, xn_ref, a_ref, s_ref, gf_ref, wup_ref, cw_ref, cb_ref, wdn_ref, o_ref, *, ctx_tiles):
    tm = x_ref.shape[1]
    halo = xp_ref.shape[1]
    nc = wup_ref.shape[0]
    t = pl.program_id(1)
    nt = pl.num_programs(1)
    x = x_ref[0]
    xe = jnp.concatenate([xp_ref[0], x, xn_ref[0]], axis=0)
    hb = _norm_mod(xe, a_ref[0], s_ref[0]).astype(bf16)
    left_ok = jnp.where(jnp.logical_and(t != 0, t != ctx_tiles), 1.0, 0.0)
    right_ok = jnp.where(jnp.logical_and(t != nt - 1, t != ctx_tiles - 1), 1.0, 0.0)

    def up(c):
        return _dot(hb, wup_ref[c])

    def conv_act(u, c):
        u = jnp.concatenate([u[:halo] * left_ok, u[halo:halo + tm], u[halo + tm:] * right_ok], axis=0)
        w = cw_ref[c]
        uc = (w[0:1] * u[halo - 1:halo - 1 + tm] + w[1:2] * u[halo:halo + tm]
              + w[2:3] * u[halo + 1:halo + 1 + tm] + cb_ref[c])
        g, v = uc[:, :FF_CHUNK], uc[:, FF_CHUNK:]
        return (g * jax.nn.sigmoid(g) * v).astype(bf16)

    acc = None
    u_next = up(0)
    for c in range(nc):
        u = u_next
        if c + 1 < nc:
            u_next = up(c + 1)
        d = _dot(conv_act(u, c), wdn_ref[c])
        acc = d if acc is None else acc + d
    o_ref[0] = x + gf_ref[0] * acc


def _ffn(x1, a_row, s_row, g_row, w_up_c, conv_w_c, conv_b_c, w_down_c, ctx_tiles):
    B, Lt, D = x1.shape
    nt = Lt // TM
    nb = a_row.shape[0] - 1
    per = TM // SUBLANES
    last = Lt // SUBLANES - 1
    row = lambda b, t: (jnp.where(t < ctx_tiles, nb, b), 0, 0)
    return pl.pallas_call(
        functools.partial(_ffn_kernel, ctx_tiles=ctx_tiles),
        grid=(B, nt),
        in_specs=[
            pl.BlockSpec((1, TM, D), lambda b, t: (b, t, 0)),
            pl.BlockSpec((1, SUBLANES, D), lambda b, t: (b, jnp.maximum(t * per - 1, 0), 0)),
            pl.BlockSpec((1, SUBLANES, D), lambda b, t: (b, jnp.minimum((t + 1) * per, last), 0)),
            pl.BlockSpec((1, 1, D), row), pl.BlockSpec((1, 1, D), row), pl.BlockSpec((1, 1, D), row),
            _const_spec(w_up_c.shape), _const_spec(conv_w_c.shape), _const_spec(conv_b_c.shape),
            _const_spec(w_down_c.shape),
        ],
        out_specs=pl.BlockSpec((1, TM, D), lambda b, t: (b, t, 0)),
        out_shape=jax.ShapeDtypeStruct((B, Lt, D), f32),
        compiler_params=_cparams("parallel", "parallel"),
        name="ffn",
    )(x1, x1, x1, a_row, s_row, g_row, w_up_c, conv_w_c, conv_b_c, w_down_c)


def _rope_table(seq, ctx_len, n):
    t = jnp.arange(seq)
    inv = ROPE_BASE ** (-jnp.arange(n, dtype=f32) / n)
    ar = (t // GRID_W).astype(f32)[None, :] * inv[:, None]
    ac = (t % GRID_W).astype(f32)[None, :] * inv[:, None]
    lat = jnp.concatenate([jnp.cos(ar), jnp.sin(ar), jnp.cos(ac), jnp.sin(ac)], axis=0)
    one, zero = jnp.ones((n, ctx_len), f32), jnp.zeros((n, ctx_len), f32)
    return jnp.concatenate([jnp.concatenate([one, zero, one, zero], axis=0), lat], axis=1)


def _chunked_ffn_weights(w_up, conv_w, conv_b, w_down):
    d_ff = w_down.shape[0]
    nc = d_ff // FF_CHUNK
    pair = lambda a: jnp.concatenate([a[..., :d_ff].reshape(a.shape[:-1] + (nc, FF_CHUNK)),
                                      a[..., d_ff:].reshape(a.shape[:-1] + (nc, FF_CHUNK))], axis=-1)
    w_up_c = jnp.moveaxis(pair(w_up), 1, 0).astype(bf16)
    conv_w_c = jnp.moveaxis(pair(conv_w), 1, 0)
    conv_b_c = pair(conv_b)[:, None, :]
    w_down_c = w_down.reshape(nc, FF_CHUNK, w_down.shape[1]).astype(bf16)
    return w_up_c, conv_w_c, conv_b_c, w_down_c


def kernel(x, c, ctx, c_ctx, w_ada, b_ada, g_mix, g_ffn, w_in, na_q_norm, na_k_norm, na_rpb, sw_q_norm, sw_k_norm,
           sw_sink, mla_q_rank_norm, mla_kv_rank_norm, w_uq, w_ukv, mla_q_norm, mla_k_norm, w_branch, w_out,
           w_up, conv_w, conv_b, w_down):
    B, S, D = x.shape
    C = ctx.shape[1]
    depth = w_ada.shape[0]
    assert C == TM and S % TM == 0 and GRID_W * NA_QROWS == TM and (S // GRID_W) * GRID_W == S
    assert S // GRID_W >= NA_KEYS // GRID_W and w_down.shape[1] % FF_CHUNK == 0
    assert (S // TM) % MLA_STEP_CHUNKS == 0
    rows = S // GRID_W

    xc = jnp.concatenate([ctx, x], axis=1)
    n_mod = -(-(B + 1) // SUBLANES) * SUBLANES
    c_all = jnp.zeros((n_mod, D), f32).at[:B].set(c).at[B].set(c_ctx)
    zero = jnp.zeros_like(g_mix)
    g_sel = jnp.stack([zero, g_mix, zero, zero, g_ffn, zero], axis=1)[:, :, None, :]
    mod = _ada(c_all, w_ada, b_ada, g_sel)

    rope_sw = _rope_table(S, C, HEAD_DIM // 4)
    rope_ml = _rope_table(S, C, MLA_ROPE // 4)
    group = SW_Q_HEADS // SW_KV_HEADS

    for l in range(depth):
        off = 0 if l < depth - 1 else 1
        m6 = mod[l, :B + 1].reshape(B + 1, 6, 1, D)
        shift_a, a_mix, gate_a, shift_f, a_ffn, gate_f = (m6[:, j] for j in range(6))
        w_in_t = w_in[l][:, :N_STREAM].T.astype(bf16)
        gains = jnp.concatenate([na_q_norm[l], na_k_norm[l], sw_q_norm[l], sw_k_norm[l], mla_q_rank_norm[l],
                                 mla_kv_rank_norm[l], mla_q_norm[l], mla_k_norm[l]])[:, None]
        (na_q, na_k, na_v, sw_q, sw_k, sw_v, m_q, m_k, m_v) = _proj(
            xc, a_mix, shift_a, w_in_t, w_uq[l].T.astype(bf16), w_ukv[l].T.astype(bf16),
            gains, rope_sw, rope_ml)
        o_na = _na(na_q, na_k, na_v, _na_bias_table(na_rpb[l], rows), off)
        sink_rows = jnp.broadcast_to((sw_sink[l] * LOG2E)[:, None, None], (SW_Q_HEADS, 1, TM))
        o_sw = _sw(sw_q, sw_k, sw_v, sink_rows, off, C)
        o_ml = _mla(m_q, m_k, m_v, off)
        x1 = _merge(xc, a_mix, shift_a, gate_a, o_na, o_sw, o_ml, w_in[l][:, N_STREAM:].astype(bf16),
                    w_branch[l].astype(bf16), w_out[l].astype(bf16), off)
        xc = _ffn(x1, a_ffn, shift_f, gate_f, *_chunked_ffn_weights(w_up[l], conv_w[l], conv_b[l], w_down[l]),
                  ctx_tiles=1 - off)
    return xc
```

```python
import functools

import jax
import jax.numpy as jnp
from jax import lax
from jax.experimental import pallas as pl
from jax.experimental.pallas import tpu as pltpu

f32 = jnp.float32
bf16 = jnp.bfloat16

GRID_W = 64
HEAD_DIM = 64
ROPE_BASE = 10000.0
NORM_EPS = 1e-6
NEG_INF = -1e30
NA_HEADS = 8
NA_WIN_ROWS = 8
NA_WIN_COLS = 16
SW_Q_HEADS = 8
SW_KV_HEADS = 2
SW_WINDOW = 128
MLA_HEADS = 8
MLA_Q_RANK = 384
MLA_KV_RANK = 256
MLA_NOPE = 64
MLA_ROPE = 32
MLA_V = 64
N_BRANCH = 3
BRANCH_W = 512
CONV_W = 3

LANES = 128
SUBLANES = 8
TM = 256
MLA_PAD = 128
MLA_STEP_CHUNKS = 4
LOCAL_PIPE = (2, 1)
MLA_PIPE = (3, 2)
FF_CHUNK = 256
NA_QROWS = TM // GRID_W
NA_KEYS = 3 * TM
BF16_ROWS = 16
V_EXT = HEAD_DIM + BF16_ROWS
LOG2E = 1.4426950408889634
VMEM_LIMIT = 56 * 1024 * 1024

_SEG = {}
_o = 0
for _name, _n in (("na_q", 512), ("na_k", 512), ("na_v", 512), ("sw_q", 512), ("sw_k", 128), ("sw_v", 128),
                  ("c_q", MLA_Q_RANK), ("c_kv", MLA_KV_RANK), ("k_r", MLA_ROPE)):
    _SEG[_name] = (_o, _o + _n)
    _o += _n
N_STREAM = _o
_GN = {}
_o = 0
for _name, _n in (("na_q", 64), ("na_k", 64), ("sw_q", 64), ("sw_k", 64), ("mla_qr", MLA_Q_RANK),
                  ("mla_kvr", MLA_KV_RANK), ("mla_q", 96), ("mla_k", 96)):
    _GN[_name] = (_o, _o + _n)
    _o += _n
N_GAIN = _o


def _dot(a, b):
    return jnp.dot(a, b, preferred_element_type=f32)


def _cparams(*sem):
    return pltpu.CompilerParams(dimension_semantics=sem, vmem_limit_bytes=VMEM_LIMIT)


def _const_spec(shape):
    nd = len(shape)
    return pl.BlockSpec(shape, lambda *_: (0,) * nd, pipeline_mode=pl.Buffered(1))


def _ada_kernel(c_ref, w_ref, b_ref, g_ref, o_ref):
    n = pl.program_id(1)
    c = c_ref[...]
    sc = (c * jax.nn.sigmoid(c)).astype(bf16)
    mod = _dot(sc, w_ref[0].astype(bf16)) + b_ref[0]
    fold = jnp.logical_or(n == 1, n == 4)
    o_ref[0] = jnp.where(fold, g_ref[0, 0] * (1.0 + mod), mod)


def _ada(c_all, w_ada, b_ada, g_sel):
    L, D, _ = w_ada.shape
    R = c_all.shape[0]
    return pl.pallas_call(
        _ada_kernel,
        grid=(L, 6),
        in_specs=[
            pl.BlockSpec((R, D), lambda l, n: (0, 0)),
            pl.BlockSpec((1, D, D), lambda l, n: (l, 0, n)),
            pl.BlockSpec((1, 1, D), lambda l, n: (l, 0, n)),
            pl.BlockSpec((1, 1, 1, D), lambda l, n: (l, n, 0, 0)),
        ],
        out_specs=pl.BlockSpec((1, R, D), lambda l, n: (l, 0, n)),
        out_shape=jax.ShapeDtypeStruct((L, R, 6 * D), f32),
        compiler_params=_cparams("parallel", "parallel"),
        name="ada",
    )(c_all, w_ada, b_ada.reshape(L, 1, 6 * D), g_sel)


def _norm_mod(x, a, s):
    ms = jnp.mean(x * x, axis=-1, keepdims=True)
    return (x * lax.rsqrt(ms + NORM_EPS)) * a + s


def _head_norm(y3, gain):
    ms = jnp.mean(y3 * y3, axis=1, keepdims=True)
    return y3 * lax.rsqrt(ms + NORM_EPS) * gain[None]


def _rope(t3, tab, lo, n):
    cr, sr, cc, sc = (tab[i * n:(i + 1) * n][None] for i in range(4))
    a, b, c, e = (t3[:, lo + i * n:lo + (i + 1) * n] for i in range(4))
    return [a * cr - b * sr, a * sr + b * cr, c * cc - e * sc, c * sc + e * cc]


def _pipelined_heads(n, scores, softmax, weighted_values, ahead, behind):
    s, pending = {}, []
    for h in range(min(ahead, n)):
        s[h] = scores(h)
    for h in range(n):
        if h + ahead < n:
            s[h + ahead] = scores(h + ahead)
        pending.append((h, softmax(h, s.pop(h))))
        if len(pending) > behind:
            weighted_values(*pending.pop(0))
    for item in pending:
        weighted_values(*item)


def _stream_tile(c_ref, x_ref, t):
    return jnp.where(t == 0, c_ref[0], x_ref[0])


def _stream_specs(c_src, x_src, off):
    shift = 0 if c_src is x_src else 1
    d = x_src.shape[2]
    return [pl.BlockSpec((1, TM, d), lambda b, i: (b, 0, 0)),
            pl.BlockSpec((1, TM, d), lambda b, i: (b, jnp.maximum(i + off - shift, 0), 0))]


def _proj_kernel(c_ref, x_ref, a_ref, s_ref, win_ref, wuq_ref, wukv_ref, gn_ref, rsw_ref, rml_ref,
                 naq_ref, nak_ref, nav_ref, swq_ref, swk_ref, swv_ref, mq_ref, mk_ref, mv_ref):
    tm = x_ref.shape[1]
    h = _norm_mod(_stream_tile(c_ref, x_ref, pl.program_id(1)), a_ref[0], s_ref[0])
    ht = h.T.astype(bf16)

    def seg(name):
        lo, hi = _SEG[name]
        return _dot(win_ref[lo:hi, :], ht)

    def gain(name):
        lo, hi = _GN[name]
        return gn_ref[lo:hi, :]

    def with_ones(v, heads):
        v3 = v.reshape(heads, HEAD_DIM, tm)
        ones = jnp.ones((heads, BF16_ROWS, tm), f32)
        return jnp.concatenate([v3, ones], axis=1).reshape(heads * V_EXT, tm).astype(bf16)

    qs = HEAD_DIM ** -0.5 * LOG2E
    q = _head_norm(seg("na_q").reshape(NA_HEADS, HEAD_DIM, tm), gain("na_q")) * qs
    naq_ref[0] = q.reshape(NA_HEADS * HEAD_DIM, tm).astype(bf16)
    k = _head_norm(seg("na_k").reshape(NA_HEADS, HEAD_DIM, tm), gain("na_k"))
    nak_ref[0] = k.reshape(NA_HEADS * HEAD_DIM, tm).T.astype(bf16)
    nav_ref[0, 0] = with_ones(seg("na_v"), NA_HEADS)

    rsw = rsw_ref[...]
    nq = HEAD_DIM // 4
    q = _head_norm(seg("sw_q").reshape(SW_Q_HEADS, HEAD_DIM, tm), gain("sw_q"))
    q = jnp.concatenate(_rope(q, rsw, 0, nq), axis=1) * qs
    swq_ref[0] = q.reshape(SW_Q_HEADS * HEAD_DIM, tm).astype(bf16)
    k = _head_norm(seg("sw_k").reshape(SW_KV_HEADS, HEAD_DIM, tm), gain("sw_k"))
    k = jnp.concatenate(_rope(k, rsw, 0, nq), axis=1)
    swk_ref[0] = k.reshape(SW_KV_HEADS * HEAD_DIM, tm).T.astype(bf16)
    v = with_ones(seg("sw_v"), SW_KV_HEADS)
    for j in range(tm // LANES):
        swv_ref[0, j] = v[:, j * LANES:(j + 1) * LANES]

    rml = rml_ref[...]
    nr = MLA_ROPE // 4
    dq = MLA_NOPE + MLA_ROPE
    pad = jnp.zeros((MLA_HEADS, MLA_PAD - dq, tm), f32)
    cq = seg("c_q")
    cq = cq * lax.rsqrt(jnp.mean(cq * cq, axis=0, keepdims=True) + NORM_EPS) * gain("mla_qr")
    mq = _head_norm(_dot(wuq_ref[...], cq.astype(bf16)).reshape(MLA_HEADS, dq, tm), gain("mla_q"))
    mq = jnp.concatenate([mq[:, :MLA_NOPE]] + _rope(mq, rml, MLA_NOPE, nr), axis=1) * (dq ** -0.5 * LOG2E)
    mq_ref[0] = jnp.concatenate([mq, pad], axis=1).reshape(MLA_HEADS * MLA_PAD, tm).astype(bf16)

    ckv = seg("c_kv")
    ckv = ckv * lax.rsqrt(jnp.mean(ckv * ckv, axis=0, keepdims=True) + NORM_EPS) * gain("mla_kvr")
    kv = _dot(wukv_ref[...], ckv.astype(bf16)).reshape(MLA_HEADS, MLA_NOPE + MLA_V, tm)
    kn, v = kv[:, :MLA_NOPE], kv[:, MLA_NOPE:]
    kr = seg("k_r")
    ss = jnp.sum(kn * kn, axis=1, keepdims=True) + jnp.sum(kr * kr, axis=0, keepdims=True)[None]
    r = lax.rsqrt(ss / dq + NORM_EPS)
    gk = gain("mla_k")
    kn = kn * r * gk[:MLA_NOPE][None]
    krh = kr[None] * r * gk[MLA_NOPE:][None]
    mk = jnp.concatenate([kn] + _rope(krh, rml, 0, nr) + [pad], axis=1)
    mk_ref[0] = mk.reshape(MLA_HEADS * MLA_PAD, tm).T.astype(bf16)
    mv_ref[0, 0] = with_ones(v.reshape(MLA_HEADS * MLA_V, tm), MLA_HEADS)


def _proj(c_src, x_src, a_row, s_row, win_t, wuq_t, wukv_t, gains, rope_sw, rope_ml):
    B, _, D = x_src.shape
    T = x_src.shape[1] // TM + (0 if c_src is x_src else 1)
    Lc = T * TM
    nb = a_row.shape[0] - 1
    row = lambda b, t: (jnp.where(t == 0, nb, b), 0, 0)
    fm = lambda n: pl.BlockSpec((1, n, TM), lambda b, t: (b, 0, t))
    tk = lambda n: pl.BlockSpec((1, TM, n), lambda b, t: (b, t, 0))
    sds = jax.ShapeDtypeStruct
    return pl.pallas_call(
        _proj_kernel,
        grid=(B, T),
        in_specs=[
            *_stream_specs(c_src, x_src, 0),
            pl.BlockSpec((1, 1, D), row),
            pl.BlockSpec((1, 1, D), row),
            _const_spec(win_t.shape), _const_spec(wuq_t.shape), _const_spec(wukv_t.shape),
            _const_spec(gains.shape),
            pl.BlockSpec((rope_sw.shape[0], TM), lambda b, t: (0, t)),
            pl.BlockSpec((rope_ml.shape[0], TM), lambda b, t: (0, t)),
        ],
        out_specs=[
            fm(512), tk(512), pl.BlockSpec((1, 1, NA_HEADS * V_EXT, TM), lambda b, t: (b, t, 0, 0)),
            fm(512), tk(128),
            pl.BlockSpec((1, TM // LANES, SW_KV_HEADS * V_EXT, LANES), lambda b, t: (b, t, 0, 0)),
            fm(MLA_HEADS * MLA_PAD), tk(MLA_HEADS * MLA_PAD),
            pl.BlockSpec((1, 1, MLA_HEADS * V_EXT, TM), lambda b, t: (b, t, 0, 0)),
        ],
        out_shape=[
            sds((B, 512, Lc), bf16), sds((B, Lc, 512), bf16), sds((B, T, NA_HEADS * V_EXT, TM), bf16),
            sds((B, 512, Lc), bf16), sds((B, Lc, 128), bf16),
            sds((B, Lc // LANES, SW_KV_HEADS * V_EXT, LANES), bf16),
            sds((B, MLA_HEADS * MLA_PAD, Lc), bf16), sds((B, Lc, MLA_HEADS * MLA_PAD), bf16),
            sds((B, T, MLA_HEADS * V_EXT, TM), bf16),
        ],
        compiler_params=_cparams("parallel", "parallel"),
        name="proj",
    )(c_src, x_src, a_row, s_row, win_t, wuq_t, wukv_t, gains, rope_sw, rope_ml)


def _na_kernel(q_ref, kc_ref, k0_ref, k1_ref, k2_ref, vc_ref, v0_ref, v1_ref, v2_ref, bias_ref, o_ref, acc_ref):
    tq = q_ref.shape[2]
    zeros = jnp.zeros((HEAD_DIM, tq), bf16)
    k_refs = (kc_ref, k0_ref, k1_ref, k2_ref)
    v_refs = (vc_ref, v0_ref, v1_ref, v2_ref)

    def scores(h):
        pair = slice((h // 2) * LANES, (h // 2 + 1) * LANES)
        qh = q_ref[0, h * HEAD_DIM:(h + 1) * HEAD_DIM, :]
        qpad = jnp.concatenate([qh, zeros] if h % 2 == 0 else [zeros, qh], axis=0)
        s = [_dot(kr[0, :, pair], qpad) for kr in k_refs]
        for i in range(3):
            s[i + 1] = s[i + 1] + bias_ref[0, h, i * TM:(i + 1) * TM, :]
        return s

    def softmax(h, s):
        m = functools.reduce(jnp.maximum, [jnp.max(x, axis=0, keepdims=True) for x in s])
        return [jnp.exp2((x - m).astype(bf16)) for x in s]

    def weighted_values(h, p):
        vrows = slice(h * V_EXT, (h + 1) * V_EXT)
        o = functools.reduce(jnp.add, [_dot(vr[0, 0, vrows, :], x) for vr, x in zip(v_refs, p)])
        acc_ref[h * HEAD_DIM:(h + 1) * HEAD_DIM, :] = o[:HEAD_DIM] / o[HEAD_DIM:HEAD_DIM + 1]

    _pipelined_heads(NA_HEADS, scores, softmax, weighted_values, *LOCAL_PIPE)
    o_ref[0] = acc_ref[...].T.astype(bf16)


def _na(q_t, k, v_c, bias, off):
    B, _, Lc = q_t.shape
    T = Lc // TM
    nq = T - off
    nblk = T - 1

    def base(i):
        return 1 + jnp.clip(i + off - 2, 0, nblk - 3)

    def btype(i):
        qt = i + off
        return jnp.where(qt == 0, 3, jnp.where(qt == 1, 0, jnp.where(qt == T - 1, 2, 1)))

    kspec = lambda j: pl.BlockSpec((1, TM, 512), lambda b, i: (b, base(i) + j, 0))
    vrows = NA_HEADS * V_EXT
    vspec = lambda j: pl.BlockSpec((1, 1, vrows, TM), lambda b, i: (b, base(i) + j, 0, 0))
    return pl.pallas_call(
        _na_kernel,
        grid=(B, nq),
        in_specs=[
            pl.BlockSpec((1, 512, TM), lambda b, i: (b, 0, i + off)),
            pl.BlockSpec((1, TM, 512), lambda b, i: (b, 0, 0)), kspec(0), kspec(1), kspec(2),
            pl.BlockSpec((1, 1, vrows, TM), lambda b, i: (b, 0, 0, 0)), vspec(0), vspec(1), vspec(2),
            pl.BlockSpec((1, NA_HEADS, NA_KEYS, TM), lambda b, i: (btype(i), 0, 0, 0)),
        ],
        out_specs=pl.BlockSpec((1, TM, 512), lambda b, i: (b, i, 0)),
        out_shape=jax.ShapeDtypeStruct((B, nq * TM, 512), bf16),
        scratch_shapes=[pltpu.VMEM((512, TM), f32)],
        compiler_params=_cparams("parallel", "arbitrary"),
        name="na_attn",
    )(q_t, k, k, k, k, v_c, v_c, v_c, v_c, bias)


def _na_bias_table(rpb, rows):
    col = jnp.arange(GRID_W)
    dc = jnp.clip(col[:, None] - col[None, :], -(NA_WIN_COLS - 1), NA_WIN_COLS - 1) + (NA_WIN_COLS - 1)
    onehot = (dc[None] == jnp.arange(2 * NA_WIN_COLS - 1)[:, None, None]).astype(f32)
    tile = jnp.einsum("hrd,dkq->hrkq", rpb.astype(f32) * LOG2E, onehot, precision=lax.Precision.HIGHEST)
    c0 = jnp.clip(col - NA_WIN_COLS // 2, 0, GRID_W - NA_WIN_COLS)
    in_win = (col[:, None] >= c0[None, :]) & (col[:, None] < c0[None, :] + NA_WIN_COLS)
    tile = jnp.where(in_win, tile, NEG_INF)
    masked = jnp.full((rpb.shape[0], GRID_W, GRID_W), NEG_INF, f32)
    krows = NA_KEYS // GRID_W

    def one(r0, ks):
        out = []
        for kr in range(ks, ks + krows):
            parts = []
            for qr in range(r0, r0 + NA_QROWS):
                start = min(max(qr - NA_WIN_ROWS // 2, 0), rows - NA_WIN_ROWS)
                parts.append(tile[:, kr - qr + NA_WIN_ROWS - 1] if start <= kr < start + NA_WIN_ROWS else masked)
            out.append(jnp.concatenate(parts, axis=2))
        return jnp.concatenate(out, axis=1)

    tabs = [one(0, 0), one(NA_QROWS, 0), one(rows - NA_QROWS, rows - krows)]
    tabs.append(jnp.full_like(tabs[0], NEG_INF))
    return jnp.stack(tabs)


def _sw_kernel(q_ref, kc_ref, k0_ref, k1_ref, k2_ref, k3_ref, vc_ref, v0_ref, v1_ref, v2_ref, v3_ref,
               sink_ref, o_ref, acc_ref, *, off, ctx_len, total_len):
    tq = q_ref.shape[2]
    qt = pl.program_id(1) + off
    group = SW_Q_HEADS // SW_KV_HEADS
    k_refs = (k0_ref, k1_ref, k2_ref, k3_ref)
    v_refs = (v0_ref, v1_ref, v2_ref, v3_ref)
    kk = lax.broadcasted_iota(jnp.int32, (LANES, tq), 0)
    qpos = qt * tq + lax.broadcasted_iota(jnp.int32, (LANES, tq), 1)
    madd = []
    for i in range(4):
        kpos = (qt * (tq // LANES) - 1 + i) * LANES + kk
        ok = (kpos >= ctx_len) & (kpos < total_len) & (jnp.abs(kpos - qpos) <= SW_WINDOW) & (qt > 0)
        madd.append(jnp.where(ok, 0.0, NEG_INF).astype(f32))
    zeros = jnp.zeros((HEAD_DIM, tq), bf16)

    def scores(h):
        qh = q_ref[0, h * HEAD_DIM:(h + 1) * HEAD_DIM, :]
        qpad = jnp.concatenate([qh, zeros] if h // group == 0 else [zeros, qh], axis=0)
        return [_dot(kc_ref[0], qpad)] + [_dot(kr[0], qpad) + madd[i] for i, kr in enumerate(k_refs)]

    def softmax(h, s):
        sink = sink_ref[h]
        m = functools.reduce(jnp.maximum, [jnp.max(x, axis=0, keepdims=True) for x in s] + [sink])
        return [jnp.exp2((x - m).astype(bf16)) for x in s], jnp.exp2(sink - m)

    def weighted_values(h, item):
        p, p_sink = item
        kv_rows = slice((h // group) * V_EXT, (h // group + 1) * V_EXT)
        parts = [_dot(vc_ref[0, j, kv_rows, :], p[0][j * LANES:(j + 1) * LANES]) for j in range(tq // LANES)]
        parts += [_dot(vr[0, 0, kv_rows, :], x) for vr, x in zip(v_refs, p[1:])]
        o = functools.reduce(jnp.add, parts)
        acc_ref[h * HEAD_DIM:(h + 1) * HEAD_DIM, :] = o[:HEAD_DIM] / (o[HEAD_DIM:HEAD_DIM + 1] + p_sink)

    _pipelined_heads(SW_Q_HEADS, scores, softmax, weighted_values, *LOCAL_PIPE)
    o_ref[0] = acc_ref[...].T.astype(bf16)


def _sw(q_t, k, v_c, sink_rows, off, ctx_len):
    B, _, Lc = q_t.shape
    T = Lc // TM
    nq = T - off
    per = TM // LANES
    lo, hi = ctx_len // LANES, Lc // LANES - 1

    def blk(i, j):
        return jnp.clip((i + off) * per - 1 + j, lo, hi)

    kspec = lambda j: pl.BlockSpec((1, LANES, 128), lambda b, i: (b, blk(i, j), 0))
    vrows = SW_KV_HEADS * V_EXT
    vspec = lambda j: pl.BlockSpec((1, 1, vrows, LANES), lambda b, i: (b, blk(i, j), 0, 0))
    kern = functools.partial(_sw_kernel, off=off, ctx_len=ctx_len, total_len=Lc)
    return pl.pallas_call(
        kern,
        grid=(B, nq),
        in_specs=[
            pl.BlockSpec((1, 512, TM), lambda b, i: (b, 0, i + off)),
            pl.BlockSpec((1, ctx_len, 128), lambda b, i: (b, 0, 0)),
            kspec(0), kspec(1), kspec(2), kspec(3),
            pl.BlockSpec((1, ctx_len // LANES, vrows, LANES), lambda b, i: (b, 0, 0, 0)),
            vspec(0), vspec(1), vspec(2), vspec(3),
            pl.BlockSpec(sink_rows.shape, lambda b, i: (0, 0, 0)),
        ],
        out_specs=pl.BlockSpec((1, TM, 512), lambda b, i: (b, i, 0)),
        out_shape=jax.ShapeDtypeStruct((B, nq * TM, 512), bf16),
        scratch_shapes=[pltpu.VMEM((512, TM), f32)],
        compiler_params=_cparams("parallel", "arbitrary"),
        name="sw_attn",
    )(q_t, k, k, k, k, k, v_c, v_c, v_c, v_c, v_c, sink_rows)


def _mla_kernel(q_ref, k_ref, v_ref, o_ref, acc_ref, out_ref, *, off):
    nchunk = v_ref.shape[1]
    nstep = (nchunk - 1) // MLA_STEP_CHUNKS
    qt = pl.program_id(1) + off
    nk = jnp.where(qt == 0, 0, nstep)
    tq = q_ref.shape[2]
    acc_ref[...] = jnp.zeros_like(acc_ref)

    def step(ms, r0, chunks):
        nrows = len(chunks) * TM

        def scores(h):
            pair = slice((h // 2) * 2 * MLA_PAD, (h // 2 + 1) * 2 * MLA_PAD)
            qh = q_ref[0, h * MLA_PAD:(h + 1) * MLA_PAD, :]
            zeros = jnp.zeros_like(qh)
            qpad = jnp.concatenate([qh, zeros] if h % 2 == 0 else [zeros, qh], axis=0)
            return [_dot(k_ref[0, pl.ds(r0 + i * TM, TM), pair], qpad) for i in range(len(chunks))]

        ms_new = [None] * MLA_HEADS

        def softmax(h, s):
            m_chunk = functools.reduce(jnp.maximum, [jnp.max(x, axis=0, keepdims=True) for x in s])
            ms_new[h] = jnp.maximum(ms[h], m_chunk)
            return jnp.exp2(ms[h] - ms_new[h]), [jnp.exp2(x - ms_new[h]).astype(bf16) for x in s]

        def weighted_values(h, item):
            alpha, p = item
            rows = slice(h * V_EXT, (h + 1) * V_EXT)
            upd = functools.reduce(jnp.add, [_dot(v_ref[0, c, rows, :], x) for c, x in zip(chunks, p)])
            acc_ref[rows, :] = alpha * acc_ref[rows, :] + upd

        _pipelined_heads(MLA_HEADS, scores, softmax, weighted_values, *MLA_PIPE)
        return tuple(ms_new)

    ms = step(tuple(jnp.full((1, tq), NEG_INF, f32) for _ in range(MLA_HEADS)), 0, [0])

    def body(i, ms):
        c0 = 1 + i * MLA_STEP_CHUNKS
        return step(ms, pl.multiple_of(c0 * TM, TM), [c0 + j for j in range(MLA_STEP_CHUNKS)])

    lax.fori_loop(0, nk, body, ms)
    for h in range(MLA_HEADS):
        num = acc_ref[h * V_EXT:h * V_EXT + MLA_V, :]
        out_ref[h * MLA_V:(h + 1) * MLA_V, :] = num / acc_ref[h * V_EXT + MLA_V:h * V_EXT + MLA_V + 1, :]
    o_ref[0] = out_ref[...].T.astype(bf16)


def _mla(q_t, k, v_c, off):
    B, _, Lc = q_t.shape
    T = Lc // TM
    nq = T - off
    return pl.pallas_call(
        functools.partial(_mla_kernel, off=off),
        grid=(B, nq),
        in_specs=[
            pl.BlockSpec((1, MLA_HEADS * MLA_PAD, TM), lambda b, i: (b, 0, i + off)),
            pl.BlockSpec((1, Lc, MLA_HEADS * MLA_PAD), lambda b, i: (b, 0, 0)),
            pl.BlockSpec((1, T, MLA_HEADS * V_EXT, TM), lambda b, i: (b, 0, 0, 0)),
        ],
        out_specs=pl.BlockSpec((1, TM, 512), lambda b, i: (b, i, 0)),
        out_shape=jax.ShapeDtypeStruct((B, nq * TM, 512), bf16),
        scratch_shapes=[pltpu.VMEM((MLA_HEADS * V_EXT, TM), f32), pltpu.VMEM((512, TM), f32)],
        compiler_params=_cparams("parallel", "arbitrary"),
        name="mla_attn",
    )(q_t, k, v_c)


def _merge_kernel(c_ref, x_ref, a_ref, s_ref, ga_ref, ona_ref, osw_ref, oml_ref, wg_ref, wb_ref, wo_ref, o_ref,
                  *, off):
    x = _stream_tile(c_ref, x_ref, pl.program_id(1) + off)
    d = x.shape[1]
    hb = _norm_mod(x, a_ref[0], s_ref[0]).astype(bf16)
    z = None
    for n, o_n in enumerate((ona_ref, osw_ref, oml_ref)):
        g = jax.nn.sigmoid(_dot(hb, wg_ref[:, n * d:(n + 1) * d]))
        y = g * _dot(o_n[0], wb_ref[n])
        z = y if z is None else z + y
    o_ref[0] = x + ga_ref[0] * _dot(z.astype(bf16), wo_ref[...])


def _merge(c_src, x_src, a_row, s_row, g_row, o_na, o_sw, o_ml, w_gate, w_branch, w_out, off):
    B, _, D = x_src.shape
    nt = x_src.shape[1] // TM + (0 if c_src is x_src else 1) - off
    nb = a_row.shape[0] - 1
    row = lambda b, i: (jnp.where(i + off == 0, nb, b), 0, 0)
    ospec = pl.BlockSpec((1, TM, BRANCH_W), lambda b, i: (b, i, 0))
    return pl.pallas_call(
        functools.partial(_merge_kernel, off=off),
        grid=(B, nt),
        in_specs=[
            *_stream_specs(c_src, x_src, off),
            pl.BlockSpec((1, 1, D), row), pl.BlockSpec((1, 1, D), row), pl.BlockSpec((1, 1, D), row),
            ospec, ospec, ospec,
            _const_spec(w_gate.shape), _const_spec(w_branch.shape), _const_spec(w_out.shape),
        ],
        out_specs=pl.BlockSpec((1, TM, D), lambda b, i: (b, i, 0)),
        out_shape=jax.ShapeDtypeStruct((B, nt * TM, D), f32),
        compiler_params=_cparams("parallel", "parallel"),
        name="merge",
    )(c_src, x_src, a_row, s_row, g_row, o_na, o_sw, o_ml, w_gate, w_branch, w_out)


def _ffn_kernel(x_ref, xp_ref, xn_ref, a_ref, s_ref, gf_ref, wup_ref, cw_ref, cb_ref, wdn_ref, o_ref, u_ref,
                *, ctx_tiles):
    tm = x_ref.shape[1]
    halo = xp_ref.shape[1]
    nc = wup_ref.shape[0]
    t = pl.program_id(1)
    nt = pl.num_programs(1)
    x = x_ref[0]
    xe = jnp.concatenate([xp_ref[0], x, xn_ref[0]], axis=0)
    hb = _norm_mod(xe, a_ref[0], s_ref[0]).astype(bf16)
    left_ok = jnp.where(jnp.logical_and(t != 0, t != ctx_tiles), 1.0, 0.0)
    right_ok = jnp.where(jnp.logical_and(t != nt - 1, t != ctx_tiles - 1), 1.0, 0.0)

    def up(c):
        u = _dot(hb, wup_ref[c])
        slot = u_ref.at[c % 2]
        slot[:halo] = u[:halo] * left_ok
        slot[halo:halo + tm] = u[halo:halo + tm]
        slot[halo + tm:] = u[halo + tm:] * right_ok

    def conv_act(c):
        slot = u_ref.at[c % 2]
        w = cw_ref[c]
        uc = (w[0:1] * slot[halo - 1:halo - 1 + tm] + w[1:2] * slot[halo:halo + tm]
              + w[2:3] * slot[halo + 1:halo + 1 + tm] + cb_ref[c])
        g, v = uc[:, :FF_CHUNK], uc[:, FF_CHUNK:]
        return (g * jax.nn.sigmoid(g) * v).astype(bf16)

    acc = None
    up(0)
    for c in range(nc):
        if c + 1 < nc:
            up(c + 1)
        d = _dot(conv_act(c), wdn_ref[c])
        acc = d if acc is None else acc + d
    o_ref[0] = x + gf_ref[0] * acc


def _ffn(x1, a_row, s_row, g_row, w_up_c, conv_w_c, conv_b_c, w_down_c, ctx_tiles):
    B, Lt, D = x1.shape
    nt = Lt // TM
    nb = a_row.shape[0] - 1
    per = TM // SUBLANES
    last = Lt // SUBLANES - 1
    row = lambda b, t: (jnp.where(t < ctx_tiles, nb, b), 0, 0)
    return pl.pallas_call(
        functools.partial(_ffn_kernel, ctx_tiles=ctx_tiles),
        grid=(B, nt),
        in_specs=[
            pl.BlockSpec((1, TM, D), lambda b, t: (b, t, 0)),
            pl.BlockSpec((1, SUBLANES, D), lambda b, t: (b, jnp.maximum(t * per - 1, 0), 0)),
            pl.BlockSpec((1, SUBLANES, D), lambda b, t: (b, jnp.minimum((t + 1) * per, last), 0)),
            pl.BlockSpec((1, 1, D), row), pl.BlockSpec((1, 1, D), row), pl.BlockSpec((1, 1, D), row),
            _const_spec(w_up_c.shape), _const_spec(conv_w_c.shape), _const_spec(conv_b_c.shape),
            _const_spec(w_down_c.shape),
        ],
        out_specs=pl.BlockSpec((1, TM, D), lambda b, t: (b, t, 0)),
        out_shape=jax.ShapeDtypeStruct((B, Lt, D), f32),
        scratch_shapes=[pltpu.VMEM((2, TM + 2 * SUBLANES, 2 * FF_CHUNK), f32)],
        compiler_params=_cparams("parallel", "arbitrary"),
        name="ffn",
    )(x1, x1, x1, a_row, s_row, g_row, w_up_c, conv_w_c, conv_b_c, w_down_c)


def _rope_table(seq, ctx_len, n):
    t = jnp.arange(seq)
    inv = ROPE_BASE ** (-jnp.arange(n, dtype=f32) / n)
    ar = (t // GRID_W).astype(f32)[None, :] * inv[:, None]
    ac = (t % GRID_W).astype(f32)[None, :] * inv[:, None]
    lat = jnp.concatenate([jnp.cos(ar), jnp.sin(ar), jnp.cos(ac), jnp.sin(ac)], axis=0)
    one, zero = jnp.ones((n, ctx_len), f32), jnp.zeros((n, ctx_len), f32)
    return jnp.concatenate([jnp.concatenate([one, zero, one, zero], axis=0), lat], axis=1)


def _chunked_ffn_weights(w_up, conv_w, conv_b, w_down):
    d_ff = w_down.shape[0]
    nc = d_ff // FF_CHUNK
    pair = lambda a: jnp.concatenate([a[..., :d_ff].reshape(a.shape[:-1] + (nc, FF_CHUNK)),
                                      a[..., d_ff:].reshape(a.shape[:-1] + (nc, FF_CHUNK))], axis=-1)
    w_up_c = jnp.moveaxis(pair(w_up), 1, 0).astype(bf16)
    conv_w_c = jnp.moveaxis(pair(conv_w), 1, 0)
    conv_b_c = pair(conv_b)[:, None, :]
    w_down_c = w_down.reshape(nc, FF_CHUNK, w_down.shape[1]).astype(bf16)
    return w_up_c, conv_w_c, conv_b_c, w_down_c


def kernel(x, c, ctx, c_ctx, w_ada, b_ada, g_mix, g_ffn, w_in, na_q_norm, na_k_norm, na_rpb, sw_q_norm, sw_k_norm,
           sw_sink, mla_q_rank_norm, mla_kv_rank_norm, w_uq, w_ukv, mla_q_norm, mla_k_norm, w_branch, w_out,
           w_up, conv_w, conv_b, w_down):
    B, S, D = x.shape
    C = ctx.shape[1]
    depth = w_ada.shape[0]
    assert C == TM and S % TM == 0 and GRID_W * NA_QROWS == TM and (S // GRID_W) * GRID_W == S
    assert S // GRID_W >= NA_KEYS // GRID_W and w_down.shape[1] % FF_CHUNK == 0
    assert (S // TM) % MLA_STEP_CHUNKS == 0
    rows = S // GRID_W

    c_src, x_src = ctx, x
    n_mod = -(-(B + 1) // SUBLANES) * SUBLANES
    c_all = jnp.zeros((n_mod, D), f32).at[:B].set(c).at[B].set(c_ctx)
    zero = jnp.zeros_like(g_mix)
    g_sel = jnp.stack([zero, g_mix, zero, zero, g_ffn, zero], axis=1)[:, :, None, :]
    mod = _ada(c_all, w_ada, b_ada, g_sel)

    rope_sw = _rope_table(S, C, HEAD_DIM // 4)
    rope_ml = _rope_table(S, C, MLA_ROPE // 4)
    group = SW_Q_HEADS // SW_KV_HEADS

    for l in range(depth):
        off = 0 if l < depth - 1 else 1
        m6 = mod[l, :B + 1].reshape(B + 1, 6, 1, D)
        shift_a, a_mix, gate_a, shift_f, a_ffn, gate_f = (m6[:, j] for j in range(6))
        w_in_t = w_in[l][:, :N_STREAM].T.astype(bf16)
        gains = jnp.concatenate([na_q_norm[l], na_k_norm[l], sw_q_norm[l], sw_k_norm[l], mla_q_rank_norm[l],
                                 mla_kv_rank_norm[l], mla_q_norm[l], mla_k_norm[l]])[:, None]
        (na_q, na_k, na_v, sw_q, sw_k, sw_v, m_q, m_k, m_v) = _proj(
            c_src, x_src, a_mix, shift_a, w_in_t, w_uq[l].T.astype(bf16), w_ukv[l].T.astype(bf16),
            gains, rope_sw, rope_ml)
        o_na = _na(na_q, na_k, na_v, _na_bias_table(na_rpb[l], rows), off)
        sink_rows = jnp.broadcast_to((sw_sink[l] * LOG2E)[:, None, None], (SW_Q_HEADS, 1, TM))
        o_sw = _sw(sw_q, sw_k, sw_v, sink_rows, off, C)
        o_ml = _mla(m_q, m_k, m_v, off)
        x1 = _merge(c_src, x_src, a_mix, shift_a, gate_a, o_na, o_sw, o_ml, w_in[l][:, N_STREAM:].astype(bf16),
                    w_branch[l].astype(bf16), w_out[l].astype(bf16), off)
        xc = _ffn(x1, a_ffn, shift_f, gate_f, *_chunked_ffn_weights(w_up[l], conv_w[l], conv_b[l], w_down[l]),
                  ctx_tiles=1 - off)
        c_src = x_src = xc
    return xc
```

```python
import functools

import jax
import jax.numpy as jnp
from jax import lax
from jax.experimental import pallas as pl
from jax.experimental.pallas import tpu as pltpu

f32 = jnp.float32
bf16 = jnp.bfloat16

GRID_W = 64
HEAD_DIM = 64
ROPE_BASE = 10000.0
NORM_EPS = 1e-6
NEG_INF = -1e30
NA_HEADS = 8
NA_WIN_ROWS = 8
NA_WIN_COLS = 16
SW_Q_HEADS = 8
SW_KV_HEADS = 2
SW_WINDOW = 128
MLA_HEADS = 8
MLA_Q_RANK = 384
MLA_KV_RANK = 256
MLA_NOPE = 64
MLA_ROPE = 32
MLA_V = 64
N_BRANCH = 3
BRANCH_W = 512
CONV_W = 3

LANES = 128
SUBLANES = 8
TM = 256
MLA_PAD = 128
MLA_STEP_CHUNKS = 4
LOCAL_PIPE = (2, 1)
MLA_PIPE = (3, 2)
FF_CHUNK = 256
FF_DOWN_BEHIND = 2
NA_QROWS = TM // GRID_W
NA_KEYS = 3 * TM
BF16_ROWS = 16
V_EXT = HEAD_DIM + BF16_ROWS
LOG2E = 1.4426950408889634
VMEM_LIMIT = 56 * 1024 * 1024

_SEG = {}
_o = 0
for _name, _n in (("na_q", 512), ("na_k", 512), ("na_v", 512), ("sw_q", 512), ("sw_k", 128), ("sw_v", 128),
                  ("c_q", MLA_Q_RANK), ("c_kv", MLA_KV_RANK), ("k_r", MLA_ROPE)):
    _SEG[_name] = (_o, _o + _n)
    _o += _n
N_STREAM = _o
_GN = {}
_o = 0
for _name, _n in (("na_q", 64), ("na_k", 64), ("sw_q", 64), ("sw_k", 64), ("mla_qr", MLA_Q_RANK),
                  ("mla_kvr", MLA_KV_RANK), ("mla_q", 96), ("mla_k", 96)):
    _GN[_name] = (_o, _o + _n)
    _o += _n
N_GAIN = _o


def _dot(a, b):
    return jnp.dot(a, b, preferred_element_type=f32)


def _cparams(*sem):
    return pltpu.CompilerParams(dimension_semantics=sem, vmem_limit_bytes=VMEM_LIMIT)


def _const_spec(shape):
    nd = len(shape)
    return pl.BlockSpec(shape, lambda *_: (0,) * nd, pipeline_mode=pl.Buffered(1))


def _ada_kernel(c_ref, w_ref, b_ref, g_ref, o_ref):
    n = pl.program_id(1)
    c = c_ref[...]
    sc = (c * jax.nn.sigmoid(c)).astype(bf16)
    mod = _dot(sc, w_ref[0].astype(bf16)) + b_ref[0]
    fold = jnp.logical_or(n == 1, n == 4)
    o_ref[0] = jnp.where(fold, g_ref[0, 0] * (1.0 + mod), mod)


def _ada(c_all, w_ada, b_ada, g_sel):
    L, D, _ = w_ada.shape
    R = c_all.shape[0]
    return pl.pallas_call(
        _ada_kernel,
        grid=(L, 6),
        in_specs=[
            pl.BlockSpec((R, D), lambda l, n: (0, 0)),
            pl.BlockSpec((1, D, D), lambda l, n: (l, 0, n)),
            pl.BlockSpec((1, 1, D), lambda l, n: (l, 0, n)),
            pl.BlockSpec((1, 1, 1, D), lambda l, n: (l, n, 0, 0)),
        ],
        out_specs=pl.BlockSpec((1, R, D), lambda l, n: (l, 0, n)),
        out_shape=jax.ShapeDtypeStruct((L, R, 6 * D), f32),
        compiler_params=_cparams("parallel", "parallel"),
        name="ada",
    )(c_all, w_ada, b_ada.reshape(L, 1, 6 * D), g_sel)


def _norm_mod(x, a, s):
    ms = jnp.mean(x * x, axis=-1, keepdims=True)
    return (x * lax.rsqrt(ms + NORM_EPS)) * a + s


def _head_norm(y3, gain):
    ms = jnp.mean(y3 * y3, axis=1, keepdims=True)
    return y3 * lax.rsqrt(ms + NORM_EPS) * gain[None]


def _rope(t3, tab, lo, n):
    cr, sr, cc, sc = (tab[i * n:(i + 1) * n][None] for i in range(4))
    a, b, c, e = (t3[:, lo + i * n:lo + (i + 1) * n] for i in range(4))
    return [a * cr - b * sr, a * sr + b * cr, c * cc - e * sc, c * sc + e * cc]


def _pipelined_heads(n, scores, softmax, weighted_values, ahead, behind):
    s, pending = {}, []
    for h in range(min(ahead, n)):
        s[h] = scores(h)
    for h in range(n):
        if h + ahead < n:
            s[h + ahead] = scores(h + ahead)
        pending.append((h, softmax(h, s.pop(h))))
        if len(pending) > behind:
            weighted_values(*pending.pop(0))
    for item in pending:
        weighted_values(*item)


def _stream_tile(c_ref, x_ref, t):
    return jnp.where(t == 0, c_ref[0], x_ref[0])


def _stream_specs(c_src, x_src, off):
    shift = 0 if c_src is x_src else 1
    d = x_src.shape[2]
    return [pl.BlockSpec((1, TM, d), lambda b, i: (b, 0, 0)),
            pl.BlockSpec((1, TM, d), lambda b, i: (b, jnp.maximum(i + off - shift, 0), 0))]


def _proj_kernel(c_ref, x_ref, a_ref, s_ref, win_ref, wuq_ref, wukv_ref, gn_ref, rsw_ref, rml_ref,
                 naq_ref, nak_ref, nav_ref, swq_ref, swk_ref, swv_ref, mq_ref, mk_ref, mv_ref):
    tm = x_ref.shape[1]
    h = _norm_mod(_stream_tile(c_ref, x_ref, pl.program_id(1)), a_ref[0], s_ref[0])
    ht = h.T.astype(bf16)

    def seg(name):
        lo, hi = _SEG[name]
        return _dot(win_ref[lo:hi, :], ht)

    def gain(name):
        lo, hi = _GN[name]
        return gn_ref[lo:hi, :]

    def with_ones(v, heads):
        v3 = v.reshape(heads, HEAD_DIM, tm)
        ones = jnp.ones((heads, BF16_ROWS, tm), f32)
        return jnp.concatenate([v3, ones], axis=1).reshape(heads * V_EXT, tm).astype(bf16)

    def rank_norm(y, name):
        return (y * lax.rsqrt(jnp.mean(y * y, axis=0, keepdims=True) + NORM_EPS) * gain(name)).astype(bf16)

    y_cq, y_ckv, kr = seg("c_q"), seg("c_kv"), seg("k_r")
    y_naq, y_nak = seg("na_q"), seg("na_k")
    mq = _dot(wuq_ref[...], rank_norm(y_cq, "mla_qr"))
    kv = _dot(wukv_ref[...], rank_norm(y_ckv, "mla_kvr"))
    y_nav, y_swq, y_swk, y_swv = seg("na_v"), seg("sw_q"), seg("sw_k"), seg("sw_v")

    qs = HEAD_DIM ** -0.5 * LOG2E
    q = _head_norm(y_naq.reshape(NA_HEADS, HEAD_DIM, tm), gain("na_q")) * qs
    naq_ref[0] = q.reshape(NA_HEADS * HEAD_DIM, tm).astype(bf16)
    k = _head_norm(y_nak.reshape(NA_HEADS, HEAD_DIM, tm), gain("na_k"))
    nak_ref[0] = k.reshape(NA_HEADS * HEAD_DIM, tm).T.astype(bf16)
    nav_ref[0, 0] = with_ones(y_nav, NA_HEADS)

    rml = rml_ref[...]
    nr = MLA_ROPE // 4
    dq = MLA_NOPE + MLA_ROPE
    pad = jnp.zeros((MLA_HEADS, MLA_PAD - dq, tm), f32)
    mq = _head_norm(mq.reshape(MLA_HEADS, dq, tm), gain("mla_q"))
    mq = jnp.concatenate([mq[:, :MLA_NOPE]] + _rope(mq, rml, MLA_NOPE, nr), axis=1) * (dq ** -0.5 * LOG2E)
    mq_ref[0] = jnp.concatenate([mq, pad], axis=1).reshape(MLA_HEADS * MLA_PAD, tm).astype(bf16)

    kv = kv.reshape(MLA_HEADS, MLA_NOPE + MLA_V, tm)
    kn, v = kv[:, :MLA_NOPE], kv[:, MLA_NOPE:]
    ss = jnp.sum(kn * kn, axis=1, keepdims=True) + jnp.sum(kr * kr, axis=0, keepdims=True)[None]
    r = lax.rsqrt(ss / dq + NORM_EPS)
    gk = gain("mla_k")
    kn = kn * r * gk[:MLA_NOPE][None]
    krh = kr[None] * r * gk[MLA_NOPE:][None]
    mk = jnp.concatenate([kn] + _rope(krh, rml, 0, nr) + [pad], axis=1)
    mk_ref[0] = mk.reshape(MLA_HEADS * MLA_PAD, tm).T.astype(bf16)
    mv_ref[0, 0] = with_ones(v.reshape(MLA_HEADS * MLA_V, tm), MLA_HEADS)

    rsw = rsw_ref[...]
    nq = HEAD_DIM // 4
    q = _head_norm(y_swq.reshape(SW_Q_HEADS, HEAD_DIM, tm), gain("sw_q"))
    q = jnp.concatenate(_rope(q, rsw, 0, nq), axis=1) * qs
    swq_ref[0] = q.reshape(SW_Q_HEADS * HEAD_DIM, tm).astype(bf16)
    k = _head_norm(y_swk.reshape(SW_KV_HEADS, HEAD_DIM, tm), gain("sw_k"))
    k = jnp.concatenate(_rope(k, rsw, 0, nq), axis=1)
    swk_ref[0] = k.reshape(SW_KV_HEADS * HEAD_DIM, tm).T.astype(bf16)
    v = with_ones(y_swv, SW_KV_HEADS)
    for j in range(tm // LANES):
        swv_ref[0, j] = v[:, j * LANES:(j + 1) * LANES]


def _proj(c_src, x_src, a_row, s_row, win_t, wuq_t, wukv_t, gains, rope_sw, rope_ml):
    B, _, D = x_src.shape
    T = x_src.shape[1] // TM + (0 if c_src is x_src else 1)
    Lc = T * TM
    nb = a_row.shape[0] - 1
    row = lambda b, t: (jnp.where(t == 0, nb, b), 0, 0)
    fm = lambda n: pl.BlockSpec((1, n, TM), lambda b, t: (b, 0, t))
    tk = lambda n: pl.BlockSpec((1, TM, n), lambda b, t: (b, t, 0))
    sds = jax.ShapeDtypeStruct
    return pl.pallas_call(
        _proj_kernel,
        grid=(B, T),
        in_specs=[
            *_stream_specs(c_src, x_src, 0),
            pl.BlockSpec((1, 1, D), row),
            pl.BlockSpec((1, 1, D), row),
            _const_spec(win_t.shape), _const_spec(wuq_t.shape), _const_spec(wukv_t.shape),
            _const_spec(gains.shape),
            pl.BlockSpec((rope_sw.shape[0], TM), lambda b, t: (0, t)),
            pl.BlockSpec((rope_ml.shape[0], TM), lambda b, t: (0, t)),
        ],
        out_specs=[
            fm(512), tk(512), pl.BlockSpec((1, 1, NA_HEADS * V_EXT, TM), lambda b, t: (b, t, 0, 0)),
            fm(512), tk(128),
            pl.BlockSpec((1, TM // LANES, SW_KV_HEADS * V_EXT, LANES), lambda b, t: (b, t, 0, 0)),
            fm(MLA_HEADS * MLA_PAD), tk(MLA_HEADS * MLA_PAD),
            pl.BlockSpec((1, 1, MLA_HEADS * V_EXT, TM), lambda b, t: (b, t, 0, 0)),
        ],
        out_shape=[
            sds((B, 512, Lc), bf16), sds((B, Lc, 512), bf16), sds((B, T, NA_HEADS * V_EXT, TM), bf16),
            sds((B, 512, Lc), bf16), sds((B, Lc, 128), bf16),
            sds((B, Lc // LANES, SW_KV_HEADS * V_EXT, LANES), bf16),
            sds((B, MLA_HEADS * MLA_PAD, Lc), bf16), sds((B, Lc, MLA_HEADS * MLA_PAD), bf16),
            sds((B, T, MLA_HEADS * V_EXT, TM), bf16),
        ],
        compiler_params=_cparams("parallel", "parallel"),
        name="proj",
    )(c_src, x_src, a_row, s_row, win_t, wuq_t, wukv_t, gains, rope_sw, rope_ml)


def _na_kernel(q_ref, kc_ref, k0_ref, k1_ref, k2_ref, vc_ref, v0_ref, v1_ref, v2_ref, bias_ref, o_ref, acc_ref):
    tq = q_ref.shape[2]
    zeros = jnp.zeros((HEAD_DIM, tq), bf16)
    k_refs = (kc_ref, k0_ref, k1_ref, k2_ref)
    v_refs = (vc_ref, v0_ref, v1_ref, v2_ref)

    def scores(h):
        pair = slice((h // 2) * LANES, (h // 2 + 1) * LANES)
        qh = q_ref[0, h * HEAD_DIM:(h + 1) * HEAD_DIM, :]
        qpad = jnp.concatenate([qh, zeros] if h % 2 == 0 else [zeros, qh], axis=0)
        s = [_dot(kr[0, :, pair], qpad) for kr in k_refs]
        for i in range(3):
            s[i + 1] = s[i + 1] + bias_ref[0, h, i * TM:(i + 1) * TM, :]
        return s

    def softmax(h, s):
        m = functools.reduce(jnp.maximum, [jnp.max(x, axis=0, keepdims=True) for x in s])
        return [jnp.exp2((x - m).astype(bf16)) for x in s]

    def weighted_values(h, p):
        vrows = slice(h * V_EXT, (h + 1) * V_EXT)
        o = functools.reduce(jnp.add, [_dot(vr[0, 0, vrows, :], x) for vr, x in zip(v_refs, p)])
        acc_ref[h * HEAD_DIM:(h + 1) * HEAD_DIM, :] = o[:HEAD_DIM] / o[HEAD_DIM:HEAD_DIM + 1]

    _pipelined_heads(NA_HEADS, scores, softmax, weighted_values, *LOCAL_PIPE)
    o_ref[0] = acc_ref[...].T.astype(bf16)


def _na(q_t, k, v_c, bias, off):
    B, _, Lc = q_t.shape
    T = Lc // TM
    nq = T - off
    nblk = T - 1

    def base(i):
        return 1 + jnp.clip(i + off - 2, 0, nblk - 3)

    def btype(i):
        qt = i + off
        return jnp.where(qt == 0, 3, jnp.where(qt == 1, 0, jnp.where(qt == T - 1, 2, 1)))

    kspec = lambda j: pl.BlockSpec((1, TM, 512), lambda b, i: (b, base(i) + j, 0))
    vrows = NA_HEADS * V_EXT
    vspec = lambda j: pl.BlockSpec((1, 1, vrows, TM), lambda b, i: (b, base(i) + j, 0, 0))
    return pl.pallas_call(
        _na_kernel,
        grid=(B, nq),
        in_specs=[
            pl.BlockSpec((1, 512, TM), lambda b, i: (b, 0, i + off)),
            pl.BlockSpec((1, TM, 512), lambda b, i: (b, 0, 0)), kspec(0), kspec(1), kspec(2),
            pl.BlockSpec((1, 1, vrows, TM), lambda b, i: (b, 0, 0, 0)), vspec(0), vspec(1), vspec(2),
            pl.BlockSpec((1, NA_HEADS, NA_KEYS, TM), lambda b, i: (btype(i), 0, 0, 0)),
        ],
        out_specs=pl.BlockSpec((1, TM, 512), lambda b, i: (b, i, 0)),
        out_shape=jax.ShapeDtypeStruct((B, nq * TM, 512), bf16),
        scratch_shapes=[pltpu.VMEM((512, TM), f32)],
        compiler_params=_cparams("parallel", "arbitrary"),
        name="na_attn",
    )(q_t, k, k, k, k, v_c, v_c, v_c, v_c, bias)


def _na_bias_table(rpb, rows):
    col = jnp.arange(GRID_W)
    dc = jnp.clip(col[:, None] - col[None, :], -(NA_WIN_COLS - 1), NA_WIN_COLS - 1) + (NA_WIN_COLS - 1)
    onehot = (dc[None] == jnp.arange(2 * NA_WIN_COLS - 1)[:, None, None]).astype(f32)
    tile = jnp.einsum("hrd,dkq->hrkq", rpb.astype(f32) * LOG2E, onehot, precision=lax.Precision.HIGHEST)
    c0 = jnp.clip(col - NA_WIN_COLS // 2, 0, GRID_W - NA_WIN_COLS)
    in_win = (col[:, None] >= c0[None, :]) & (col[:, None] < c0[None, :] + NA_WIN_COLS)
    tile = jnp.where(in_win, tile, NEG_INF)
    masked = jnp.full((rpb.shape[0], GRID_W, GRID_W), NEG_INF, f32)
    krows = NA_KEYS // GRID_W

    def one(r0, ks):
        out = []
        for kr in range(ks, ks + krows):
            parts = []
            for qr in range(r0, r0 + NA_QROWS):
                start = min(max(qr - NA_WIN_ROWS // 2, 0), rows - NA_WIN_ROWS)
                parts.append(tile[:, kr - qr + NA_WIN_ROWS - 1] if start <= kr < start + NA_WIN_ROWS else masked)
            out.append(jnp.concatenate(parts, axis=2))
        return jnp.concatenate(out, axis=1)

    tabs = [one(0, 0), one(NA_QROWS, 0), one(rows - NA_QROWS, rows - krows)]
    tabs.append(jnp.full_like(tabs[0], NEG_INF))
    return jnp.stack(tabs)


def _sw_kernel(q_ref, kc_ref, k0_ref, k1_ref, k2_ref, k3_ref, vc_ref, v0_ref, v1_ref, v2_ref, v3_ref,
               sink_ref, o_ref, acc_ref, *, off, ctx_len, total_len):
    tq = q_ref.shape[2]
    qt = pl.program_id(1) + off
    group = SW_Q_HEADS // SW_KV_HEADS
    k_refs = (k0_ref, k1_ref, k2_ref, k3_ref)
    v_refs = (v0_ref, v1_ref, v2_ref, v3_ref)
    kk = lax.broadcasted_iota(jnp.int32, (LANES, tq), 0)
    qpos = qt * tq + lax.broadcasted_iota(jnp.int32, (LANES, tq), 1)
    madd = []
    for i in range(4):
        kpos = (qt * (tq // LANES) - 1 + i) * LANES + kk
        ok = (kpos >= ctx_len) & (kpos < total_len) & (jnp.abs(kpos - qpos) <= SW_WINDOW) & (qt > 0)
        madd.append(jnp.where(ok, 0.0, NEG_INF).astype(f32))
    zeros = jnp.zeros((HEAD_DIM, tq), bf16)

    def scores(h):
        qh = q_ref[0, h * HEAD_DIM:(h + 1) * HEAD_DIM, :]
        qpad = jnp.concatenate([qh, zeros] if h // group == 0 else [zeros, qh], axis=0)
        return [_dot(kc_ref[0], qpad)] + [_dot(kr[0], qpad) + madd[i] for i, kr in enumerate(k_refs)]

    def softmax(h, s):
        sink = sink_ref[h]
        m = functools.reduce(jnp.maximum, [jnp.max(x, axis=0, keepdims=True) for x in s] + [sink])
        return [jnp.exp2((x - m).astype(bf16)) for x in s], jnp.exp2(sink - m)

    def weighted_values(h, item):
        p, p_sink = item
        kv_rows = slice((h // group) * V_EXT, (h // group + 1) * V_EXT)
        parts = [_dot(vc_ref[0, j, kv_rows, :], p[0][j * LANES:(j + 1) * LANES]) for j in range(tq // LANES)]
        parts += [_dot(vr[0, 0, kv_rows, :], x) for vr, x in zip(v_refs, p[1:])]
        o = functools.reduce(jnp.add, parts)
        acc_ref[h * HEAD_DIM:(h + 1) * HEAD_DIM, :] = o[:HEAD_DIM] / (o[HEAD_DIM:HEAD_DIM + 1] + p_sink)

    _pipelined_heads(SW_Q_HEADS, scores, softmax, weighted_values, *LOCAL_PIPE)
    o_ref[0] = acc_ref[...].T.astype(bf16)


def _sw(q_t, k, v_c, sink_rows, off, ctx_len):
    B, _, Lc = q_t.shape
    T = Lc // TM
    nq = T - off
    per = TM // LANES
    lo, hi = ctx_len // LANES, Lc // LANES - 1

    def blk(i, j):
        return jnp.clip((i + off) * per - 1 + j, lo, hi)

    kspec = lambda j: pl.BlockSpec((1, LANES, 128), lambda b, i: (b, blk(i, j), 0))
    vrows = SW_KV_HEADS * V_EXT
    vspec = lambda j: pl.BlockSpec((1, 1, vrows, LANES), lambda b, i: (b, blk(i, j), 0, 0))
    kern = functools.partial(_sw_kernel, off=off, ctx_len=ctx_len, total_len=Lc)
    return pl.pallas_call(
        kern,
        grid=(B, nq),
        in_specs=[
            pl.BlockSpec((1, 512, TM), lambda b, i: (b, 0, i + off)),
            pl.BlockSpec((1, ctx_len, 128), lambda b, i: (b, 0, 0)),
            kspec(0), kspec(1), kspec(2), kspec(3),
            pl.BlockSpec((1, ctx_len // LANES, vrows, LANES), lambda b, i: (b, 0, 0, 0)),
            vspec(0), vspec(1), vspec(2), vspec(3),
            pl.BlockSpec(sink_rows.shape, lambda b, i: (0, 0, 0)),
        ],
        out_specs=pl.BlockSpec((1, TM, 512), lambda b, i: (b, i, 0)),
        out_shape=jax.ShapeDtypeStruct((B, nq * TM, 512), bf16),
        scratch_shapes=[pltpu.VMEM((512, TM), f32)],
        compiler_params=_cparams("parallel", "arbitrary"),
        name="sw_attn",
    )(q_t, k, k, k, k, k, v_c, v_c, v_c, v_c, v_c, sink_rows)


def _mla_kernel(q_ref, k_ref, v_ref, o_ref, acc_ref, out_ref, *, off):
    nchunk = v_ref.shape[1]
    nstep = (nchunk - 1) // MLA_STEP_CHUNKS
    qt = pl.program_id(1) + off
    nk = jnp.where(qt == 0, 0, nstep)
    tq = q_ref.shape[2]
    acc_ref[...] = jnp.zeros_like(acc_ref)

    def step(ms, r0, chunks):
        nrows = len(chunks) * TM

        def scores(h):
            pair = slice((h // 2) * 2 * MLA_PAD, (h // 2 + 1) * 2 * MLA_PAD)
            qh = q_ref[0, h * MLA_PAD:(h + 1) * MLA_PAD, :]
            zeros = jnp.zeros_like(qh)
            qpad = jnp.concatenate([qh, zeros] if h % 2 == 0 else [zeros, qh], axis=0)
            return [_dot(k_ref[0, pl.ds(r0 + i * TM, TM), pair], qpad) for i in range(len(chunks))]

        ms_new = [None] * MLA_HEADS

        def softmax(h, s):
            m_chunk = functools.reduce(jnp.maximum, [jnp.max(x, axis=0, keepdims=True) for x in s])
            ms_new[h] = jnp.maximum(ms[h], m_chunk)
            return jnp.exp2(ms[h] - ms_new[h]), [jnp.exp2(x - ms_new[h]).astype(bf16) for x in s]

        def weighted_values(h, item):
            alpha, p = item
            rows = slice(h * V_EXT, (h + 1) * V_EXT)
            upd = functools.reduce(jnp.add, [_dot(v_ref[0, c, rows, :], x) for c, x in zip(chunks, p)])
            acc_ref[rows, :] = alpha * acc_ref[rows, :] + upd

        _pipelined_heads(MLA_HEADS, scores, softmax, weighted_values, *MLA_PIPE)
        return tuple(ms_new)

    ms = step(tuple(jnp.full((1, tq), NEG_INF, f32) for _ in range(MLA_HEADS)), 0, [0])

    def body(i, ms):
        c0 = 1 + i * MLA_STEP_CHUNKS
        return step(ms, pl.multiple_of(c0 * TM, TM), [c0 + j for j in range(MLA_STEP_CHUNKS)])

    lax.fori_loop(0, nk, body, ms)
    for h in range(MLA_HEADS):
        num = acc_ref[h * V_EXT:h * V_EXT + MLA_V, :]
        out_ref[h * MLA_V:(h + 1) * MLA_V, :] = num / acc_ref[h * V_EXT + MLA_V:h * V_EXT + MLA_V + 1, :]
    o_ref[0] = out_ref[...].T.astype(bf16)


def _mla(q_t, k, v_c, off):
    B, _, Lc = q_t.shape
    T = Lc // TM
    nq = T - off
    return pl.pallas_call(
        functools.partial(_mla_kernel, off=off),
        grid=(B, nq),
        in_specs=[
            pl.BlockSpec((1, MLA_HEADS * MLA_PAD, TM), lambda b, i: (b, 0, i + off)),
            pl.BlockSpec((1, Lc, MLA_HEADS * MLA_PAD), lambda b, i: (b, 0, 0)),
            pl.BlockSpec((1, T, MLA_HEADS * V_EXT, TM), lambda b, i: (b, 0, 0, 0)),
        ],
        out_specs=pl.BlockSpec((1, TM, 512), lambda b, i: (b, i, 0)),
        out_shape=jax.ShapeDtypeStruct((B, nq * TM, 512), bf16),
        scratch_shapes=[pltpu.VMEM((MLA_HEADS * V_EXT, TM), f32), pltpu.VMEM((512, TM), f32)],
        compiler_params=_cparams("parallel", "arbitrary"),
        name="mla_attn",
    )(q_t, k, v_c)


def _merge_kernel(c_ref, x_ref, a_ref, s_ref, ga_ref, ona_ref, osw_ref, oml_ref, wg_ref, wb_ref, wo_ref, o_ref,
                  *, off):
    x = _stream_tile(c_ref, x_ref, pl.program_id(1) + off)
    d = x.shape[1]
    hb = _norm_mod(x, a_ref[0], s_ref[0]).astype(bf16)
    z = None
    for n, o_n in enumerate((ona_ref, osw_ref, oml_ref)):
        g = jax.nn.sigmoid(_dot(hb, wg_ref[:, n * d:(n + 1) * d]))
        y = g * _dot(o_n[0], wb_ref[n])
        z = y if z is None else z + y
    o_ref[0] = x + ga_ref[0] * _dot(z.astype(bf16), wo_ref[...])


def _merge(c_src, x_src, a_row, s_row, g_row, o_na, o_sw, o_ml, w_gate, w_branch, w_out, off):
    B, _, D = x_src.shape
    nt = x_src.shape[1] // TM + (0 if c_src is x_src else 1) - off
    nb = a_row.shape[0] - 1
    row = lambda b, i: (jnp.where(i + off == 0, nb, b), 0, 0)
    ospec = pl.BlockSpec((1, TM, BRANCH_W), lambda b, i: (b, i, 0))
    return pl.pallas_call(
        functools.partial(_merge_kernel, off=off),
        grid=(B, nt),
        in_specs=[
            *_stream_specs(c_src, x_src, off),
            pl.BlockSpec((1, 1, D), row), pl.BlockSpec((1, 1, D), row), pl.BlockSpec((1, 1, D), row),
            ospec, ospec, ospec,
            _const_spec(w_gate.shape), _const_spec(w_branch.shape), _const_spec(w_out.shape),
        ],
        out_specs=pl.BlockSpec((1, TM, D), lambda b, i: (b, i, 0)),
        out_shape=jax.ShapeDtypeStruct((B, nt * TM, D), f32),
        compiler_params=_cparams("parallel", "parallel"),
        name="merge",
    )(c_src, x_src, a_row, s_row, g_row, o_na, o_sw, o_ml, w_gate, w_branch, w_out)


def _ffn_kernel(x_ref, xp_ref, xn_ref, a_ref, s_ref, gf_ref, wup_ref, cw_ref, cb_ref, wdn_ref, o_ref, u_ref,
                *, ctx_tiles):
    tm = x_ref.shape[1]
    halo = xp_ref.shape[1]
    nc = wup_ref.shape[0]
    t = pl.program_id(1)
    nt = pl.num_programs(1)
    x = x_ref[0]
    xe = jnp.concatenate([xp_ref[0], x, xn_ref[0]], axis=0)
    hb = _norm_mod(xe, a_ref[0], s_ref[0]).astype(bf16)
    left_ok = jnp.where(jnp.logical_and(t != 0, t != ctx_tiles), 1.0, 0.0)
    right_ok = jnp.where(jnp.logical_and(t != nt - 1, t != ctx_tiles - 1), 1.0, 0.0)

    def up(c):
        u = _dot(hb, wup_ref[c])
        slot = u_ref.at[c % 2]
        slot[:halo] = u[:halo] * left_ok
        slot[halo:halo + tm] = u[halo:halo + tm]
        slot[halo + tm:] = u[halo + tm:] * right_ok

    def conv_act(c):
        slot = u_ref.at[c % 2]
        w = cw_ref[c]
        uc = (w[0:1] * slot[halo - 1:halo - 1 + tm] + w[1:2] * slot[halo:halo + tm]
              + w[2:3] * slot[halo + 1:halo + 1 + tm] + cb_ref[c])
        g, v = uc[:, :FF_CHUNK], uc[:, FF_CHUNK:]
        return (g * jax.nn.sigmoid(g) * v).astype(bf16)

    acc = jnp.zeros(x.shape, f32)
    pending = []
    up(0)
    for c in range(nc):
        if c + 1 < nc:
            up(c + 1)
        pending.append((c, conv_act(c)))
        if len(pending) > FF_DOWN_BEHIND:
            j, act = pending.pop(0)
            acc = acc + _dot(act, wdn_ref[j])
    for j, act in pending:
        acc = acc + _dot(act, wdn_ref[j])
    o_ref[0] = x + gf_ref[0] * acc


def _ffn(x1, a_row, s_row, g_row, w_up_c, conv_w_c, conv_b_c, w_down_c, ctx_tiles):
    B, Lt, D = x1.shape
    nt = Lt // TM
    nb = a_row.shape[0] - 1
    per = TM // SUBLANES
    last = Lt // SUBLANES - 1
    row = lambda b, t: (jnp.where(t < ctx_tiles, nb, b), 0, 0)
    return pl.pallas_call(
        functools.partial(_ffn_kernel, ctx_tiles=ctx_tiles),
        grid=(B, nt),
        in_specs=[
            pl.BlockSpec((1, TM, D), lambda b, t: (b, t, 0)),
            pl.BlockSpec((1, SUBLANES, D), lambda b, t: (b, jnp.maximum(t * per - 1, 0), 0)),
            pl.BlockSpec((1, SUBLANES, D), lambda b, t: (b, jnp.minimum((t + 1) * per, last), 0)),
            pl.BlockSpec((1, 1, D), row), pl.BlockSpec((1, 1, D), row), pl.BlockSpec((1, 1, D), row),
            _const_spec(w_up_c.shape), _const_spec(conv_w_c.shape), _const_spec(conv_b_c.shape),
            _const_spec(w_down_c.shape),
        ],
        out_specs=pl.BlockSpec((1, TM, D), lambda b, t: (b, t, 0)),
        out_shape=jax.ShapeDtypeStruct((B, Lt, D), f32),
        scratch_shapes=[pltpu.VMEM((2, TM + 2 * SUBLANES, 2 * FF_CHUNK), f32)],
        compiler_params=_cparams("parallel", "arbitrary"),
        name="ffn",
    )(x1, x1, x1, a_row, s_row, g_row, w_up_c, conv_w_c, conv_b_c, w_down_c)


def _rope_table(seq, ctx_len, n):
    t = jnp.arange(seq)
    inv = ROPE_BASE ** (-jnp.arange(n, dtype=f32) / n)
    ar = (t // GRID_W).astype(f32)[None, :] * inv[:, None]
    ac = (t % GRID_W).astype(f32)[None, :] * inv[:, None]
    lat = jnp.concatenate([jnp.cos(ar), jnp.sin(ar), jnp.cos(ac), jnp.sin(ac)], axis=0)
    one, zero = jnp.ones((n, ctx_len), f32), jnp.zeros((n, ctx_len), f32)
    return jnp.concatenate([jnp.concatenate([one, zero, one, zero], axis=0), lat], axis=1)


def _chunked_ffn_weights(w_up, conv_w, conv_b, w_down):
    d_ff = w_down.shape[0]
    nc = d_ff // FF_CHUNK
    pair = lambda a: jnp.concatenate([a[..., :d_ff].reshape(a.shape[:-1] + (nc, FF_CHUNK)),
                                      a[..., d_ff:].reshape(a.shape[:-1] + (nc, FF_CHUNK))], axis=-1)
    w_up_c = jnp.moveaxis(pair(w_up), 1, 0).astype(bf16)
    conv_w_c = jnp.moveaxis(pair(conv_w), 1, 0)
    conv_b_c = pair(conv_b)[:, None, :]
    w_down_c = w_down.reshape(nc, FF_CHUNK, w_down.shape[1]).astype(bf16)
    return w_up_c, conv_w_c, conv_b_c, w_down_c


def kernel(x, c, ctx, c_ctx, w_ada, b_ada, g_mix, g_ffn, w_in, na_q_norm, na_k_norm, na_rpb, sw_q_norm, sw_k_norm,
           sw_sink, mla_q_rank_norm, mla_kv_rank_norm, w_uq, w_ukv, mla_q_norm, mla_k_norm, w_branch, w_out,
           w_up, conv_w, conv_b, w_down):
    B, S, D = x.shape
    C = ctx.shape[1]
    depth = w_ada.shape[0]
    assert C == TM and S % TM == 0 and GRID_W * NA_QROWS == TM and (S // GRID_W) * GRID_W == S
    assert S // GRID_W >= NA_KEYS // GRID_W and w_down.shape[1] % FF_CHUNK == 0
    assert (S // TM) % MLA_STEP_CHUNKS == 0
    rows = S // GRID_W

    c_src, x_src = ctx, x
    n_mod = -(-(B + 1) // SUBLANES) * SUBLANES
    c_all = jnp.zeros((n_mod, D), f32).at[:B].set(c).at[B].set(c_ctx)
    zero = jnp.zeros_like(g_mix)
    g_sel = jnp.stack([zero, g_mix, zero, zero, g_ffn, zero], axis=1)[:, :, None, :]
    mod = _ada(c_all, w_ada, b_ada, g_sel)

    rope_sw = _rope_table(S, C, HEAD_DIM // 4)
    rope_ml = _rope_table(S, C, MLA_ROPE // 4)
    group = SW_Q_HEADS // SW_KV_HEADS

    for l in range(depth):
        off = 0 if l < depth - 1 else 1
        m6 = mod[l, :B + 1].reshape(B + 1, 6, 1, D)
        shift_a, a_mix, gate_a, shift_f, a_ffn, gate_f = (m6[:, j] for j in range(6))
        w_in_t = w_in[l][:, :N_STREAM].T.astype(bf16)
        gains = jnp.concatenate([na_q_norm[l], na_k_norm[l], sw_q_norm[l], sw_k_norm[l], mla_q_rank_norm[l],
                                 mla_kv_rank_norm[l], mla_q_norm[l], mla_k_norm[l]])[:, None]
        (na_q, na_k, na_v, sw_q, sw_k, sw_v, m_q, m_k, m_v) = _proj(
            c_src, x_src, a_mix, shift_a, w_in_t, w_uq[l].T.astype(bf16), w_ukv[l].T.astype(bf16),
            gains, rope_sw, rope_ml)
        o_na = _na(na_q, na_k, na_v, _na_bias_table(na_rpb[l], rows), off)
        sink_rows = jnp.broadcast_to((sw_sink[l] * LOG2E)[:, None, None], (SW_Q_HEADS, 1, TM))
        o_sw = _sw(sw_q, sw_k, sw_v, sink_rows, off, C)
        o_ml = _mla(m_q, m_k, m_v, off)
        x1 = _merge(c_src, x_src, a_mix, shift_a, gate_a, o_na, o_sw, o_ml, w_in[l][:, N_STREAM:].astype(bf16),
                    w_branch[l].astype(bf16), w_out[l].astype(bf16), off)
        xc = _ffn(x1, a_ffn, shift_f, gate_f, *_chunked_ffn_weights(w_up[l], conv_w[l], conv_b[l], w_down[l]),
                  ctx_tiles=1 - off)
        c_src = x_src = xc
    return xc
```

```python
import functools

import jax
import jax.numpy as jnp
from jax import lax
from jax.experimental import pallas as pl
from jax.experimental.pallas import tpu as pltpu

f32 = jnp.float32
bf16 = jnp.bfloat16

GRID_W = 64
HEAD_DIM = 64
ROPE_BASE = 10000.0
NORM_EPS = 1e-6
NEG_INF = -1e30
NA_HEADS = 8
NA_WIN_ROWS = 8
NA_WIN_COLS = 16
SW_Q_HEADS = 8
SW_KV_HEADS = 2
SW_WINDOW = 128
MLA_HEADS = 8
MLA_Q_RANK = 384
MLA_KV_RANK = 256
MLA_NOPE = 64
MLA_ROPE = 32
MLA_V = 64
N_BRANCH = 3
BRANCH_W = 512
CONV_W = 3

LANES = 128
SUBLANES = 8
TM = 256
MLA_PAD = 128
MLA_STEP_CHUNKS = 4
LOCAL_PIPE = (2, 1)
MLA_PIPE = (3, 2)
FF_CHUNK = 256
FF_DOWN_BEHIND = 2
NA_QROWS = TM // GRID_W
NA_KEYS = 3 * TM
BF16_ROWS = 16
V_EXT = HEAD_DIM + BF16_ROWS
LOG2E = 1.4426950408889634
VMEM_LIMIT = 56 * 1024 * 1024

_SEG = {}
_o = 0
NA_W = NA_HEADS * HEAD_DIM
SW_Q_W = SW_Q_HEADS * HEAD_DIM
SW_KV_W = SW_KV_HEADS * HEAD_DIM
MLA_W = MLA_HEADS * MLA_PAD
for _name, _n in (("na_q", NA_W), ("na_k", NA_W), ("na_v", NA_W), ("sw_q", SW_Q_W), ("sw_k", SW_KV_W),
                  ("sw_v", SW_KV_W), ("c_q", MLA_Q_RANK), ("c_kv", MLA_KV_RANK), ("k_r", MLA_ROPE)):
    _SEG[_name] = (_o, _o + _n)
    _o += _n
N_STREAM = _o
_GN = {}
_o = 0
for _name, _n in (("na_q", 64), ("na_k", 64), ("sw_q", 64), ("sw_k", 64), ("mla_qr", MLA_Q_RANK),
                  ("mla_kvr", MLA_KV_RANK), ("mla_q", 96), ("mla_k", 96)):
    _GN[_name] = (_o, _o + _n)
    _o += _n
N_GAIN = _o


def _dot(a, b):
    return jnp.dot(a, b, preferred_element_type=f32)


def _cparams(*sem):
    return pltpu.CompilerParams(dimension_semantics=sem, vmem_limit_bytes=VMEM_LIMIT)


def _const_spec(shape):
    nd = len(shape)
    return pl.BlockSpec(shape, lambda *_: (0,) * nd, pipeline_mode=pl.Buffered(1))


def _ada_kernel(c_ref, w_ref, b_ref, g_ref, o_ref):
    n = pl.program_id(1)
    c = c_ref[...]
    sc = (c * jax.nn.sigmoid(c)).astype(bf16)
    mod = _dot(sc, w_ref[0].astype(bf16)) + b_ref[0]
    fold = jnp.logical_or(n == 1, n == 4)
    o_ref[0] = jnp.where(fold, g_ref[0, 0] * (1.0 + mod), mod)


def _ada(c_all, w_ada, b_ada, g_sel):
    L, D, _ = w_ada.shape
    R = c_all.shape[0]
    return pl.pallas_call(
        _ada_kernel,
        grid=(L, 6),
        in_specs=[
            pl.BlockSpec((R, D), lambda l, n: (0, 0)),
            pl.BlockSpec((1, D, D), lambda l, n: (l, 0, n)),
            pl.BlockSpec((1, 1, D), lambda l, n: (l, 0, n)),
            pl.BlockSpec((1, 1, 1, D), lambda l, n: (l, n, 0, 0)),
        ],
        out_specs=pl.BlockSpec((1, R, D), lambda l, n: (l, 0, n)),
        out_shape=jax.ShapeDtypeStruct((L, R, 6 * D), f32),
        compiler_params=_cparams("parallel", "parallel"),
        name="ada",
    )(c_all, w_ada, b_ada.reshape(L, 1, 6 * D), g_sel)


def _norm_mod(x, a, s):
    ms = jnp.mean(x * x, axis=-1, keepdims=True)
    return (x * lax.rsqrt(ms + NORM_EPS)) * a + s


def _head_norm(y3, gain):
    ms = jnp.mean(y3 * y3, axis=1, keepdims=True)
    return y3 * lax.rsqrt(ms + NORM_EPS) * gain[None]


def _rope(t3, tab, lo, n):
    cr, sr, cc, sc = (tab[i * n:(i + 1) * n][None] for i in range(4))
    a, b, c, e = (t3[:, lo + i * n:lo + (i + 1) * n] for i in range(4))
    return [a * cr - b * sr, a * sr + b * cr, c * cc - e * sc, c * sc + e * cc]


def _pipelined_heads(n, scores, softmax, weighted_values, ahead, behind):
    s, pending = {}, []
    for h in range(min(ahead, n)):
        s[h] = scores(h)
    for h in range(n):
        if h + ahead < n:
            s[h + ahead] = scores(h + ahead)
        pending.append((h, softmax(h, s.pop(h))))
        if len(pending) > behind:
            weighted_values(*pending.pop(0))
    for item in pending:
        weighted_values(*item)


def _stream_tile(c_ref, x_ref, t):
    return jnp.where(t == 0, c_ref[0], x_ref[0])


def _stream_specs(c_src, x_src, off):
    shift = 0 if c_src is x_src else 1
    d = x_src.shape[2]
    return [pl.BlockSpec((1, TM, d), lambda b, i: (b, 0, 0)),
            pl.BlockSpec((1, TM, d), lambda b, i: (b, jnp.maximum(i + off - shift, 0), 0))]


def _proj_kernel(c_ref, x_ref, a_ref, s_ref, win_ref, wuq_ref, wukv_ref, gn_ref, rsw_ref, rml_ref,
                 naq_ref, nak_ref, nav_ref, swq_ref, swk_ref, swv_ref, mq_ref, mk_ref, mv_ref):
    tm = x_ref.shape[1]
    h = _norm_mod(_stream_tile(c_ref, x_ref, pl.program_id(1)), a_ref[0], s_ref[0])
    ht = h.T.astype(bf16)

    def seg(name):
        lo, hi = _SEG[name]
        return _dot(win_ref[lo:hi, :], ht)

    def gain(name):
        lo, hi = _GN[name]
        return gn_ref[lo:hi, :]

    def with_ones(v, heads):
        v3 = v.reshape(heads, HEAD_DIM, tm)
        ones = jnp.ones((heads, BF16_ROWS, tm), f32)
        return jnp.concatenate([v3, ones], axis=1).reshape(heads * V_EXT, tm).astype(bf16)

    def rank_norm(y, name):
        return (y * lax.rsqrt(jnp.mean(y * y, axis=0, keepdims=True) + NORM_EPS) * gain(name)).astype(bf16)

    y_cq, y_ckv, kr = seg("c_q"), seg("c_kv"), seg("k_r")
    y_naq, y_nak = seg("na_q"), seg("na_k")
    mq = _dot(wuq_ref[...], rank_norm(y_cq, "mla_qr"))
    kv = _dot(wukv_ref[...], rank_norm(y_ckv, "mla_kvr"))
    y_nav, y_swq, y_swk, y_swv = seg("na_v"), seg("sw_q"), seg("sw_k"), seg("sw_v")

    qs = HEAD_DIM ** -0.5 * LOG2E
    q = _head_norm(y_naq.reshape(NA_HEADS, HEAD_DIM, tm), gain("na_q")) * qs
    naq_ref[0] = q.reshape(NA_HEADS * HEAD_DIM, tm).astype(bf16)
    k = _head_norm(y_nak.reshape(NA_HEADS, HEAD_DIM, tm), gain("na_k"))
    nak_ref[0] = k.reshape(NA_HEADS * HEAD_DIM, tm).T.astype(bf16)
    nav_ref[0, 0] = with_ones(y_nav, NA_HEADS)

    rml = rml_ref[...]
    nr = MLA_ROPE // 4
    dq = MLA_NOPE + MLA_ROPE
    pad = jnp.zeros((MLA_HEADS, MLA_PAD - dq, tm), f32)
    mq = _head_norm(mq.reshape(MLA_HEADS, dq, tm), gain("mla_q"))
    mq = jnp.concatenate([mq[:, :MLA_NOPE]] + _rope(mq, rml, MLA_NOPE, nr), axis=1) * (dq ** -0.5 * LOG2E)
    mq_ref[0] = jnp.concatenate([mq, pad], axis=1).reshape(MLA_HEADS * MLA_PAD, tm).astype(bf16)

    kv = kv.reshape(MLA_HEADS, MLA_NOPE + MLA_V, tm)
    kn, v = kv[:, :MLA_NOPE], kv[:, MLA_NOPE:]
    ss = jnp.sum(kn * kn, axis=1, keepdims=True) + jnp.sum(kr * kr, axis=0, keepdims=True)[None]
    r = lax.rsqrt(ss / dq + NORM_EPS)
    gk = gain("mla_k")
    kn = kn * r * gk[:MLA_NOPE][None]
    krh = kr[None] * r * gk[MLA_NOPE:][None]
    mk = jnp.concatenate([kn] + _rope(krh, rml, 0, nr) + [pad], axis=1)
    mk_ref[0] = mk.reshape(MLA_HEADS * MLA_PAD, tm).T.astype(bf16)
    mv_ref[0, 0] = with_ones(v.reshape(MLA_HEADS * MLA_V, tm), MLA_HEADS)

    rsw = rsw_ref[...]
    nq = HEAD_DIM // 4
    q = _head_norm(y_swq.reshape(SW_Q_HEADS, HEAD_DIM, tm), gain("sw_q"))
    q = jnp.concatenate(_rope(q, rsw, 0, nq), axis=1) * qs
    swq_ref[0] = q.reshape(SW_Q_HEADS * HEAD_DIM, tm).astype(bf16)
    k = _head_norm(y_swk.reshape(SW_KV_HEADS, HEAD_DIM, tm), gain("sw_k"))
    k = jnp.concatenate(_rope(k, rsw, 0, nq), axis=1)
    swk_ref[0] = k.reshape(SW_KV_HEADS * HEAD_DIM, tm).T.astype(bf16)
    v = with_ones(y_swv, SW_KV_HEADS)
    for j in range(tm // LANES):
        swv_ref[0, j] = v[:, j * LANES:(j + 1) * LANES]


def _proj(c_src, x_src, a_row, s_row, win_t, wuq_t, wukv_t, gains, rope_sw, rope_ml):
    B, _, D = x_src.shape
    T = x_src.shape[1] // TM + (0 if c_src is x_src else 1)
    Lc = T * TM
    nb = a_row.shape[0] - 1
    row = lambda b, t: (jnp.where(t == 0, nb, b), 0, 0)
    fm = lambda n: pl.BlockSpec((1, n, TM), lambda b, t: (b, 0, t))
    tk = lambda n: pl.BlockSpec((1, TM, n), lambda b, t: (b, t, 0))
    sds = jax.ShapeDtypeStruct
    return pl.pallas_call(
        _proj_kernel,
        grid=(B, T),
        in_specs=[
            *_stream_specs(c_src, x_src, 0),
            pl.BlockSpec((1, 1, D), row),
            pl.BlockSpec((1, 1, D), row),
            _const_spec(win_t.shape), _const_spec(wuq_t.shape), _const_spec(wukv_t.shape),
            _const_spec(gains.shape),
            pl.BlockSpec((rope_sw.shape[0], TM), lambda b, t: (0, t)),
            pl.BlockSpec((rope_ml.shape[0], TM), lambda b, t: (0, t)),
        ],
        out_specs=[
            fm(NA_W), tk(NA_W), pl.BlockSpec((1, 1, NA_HEADS * V_EXT, TM), lambda b, t: (b, t, 0, 0)),
            fm(SW_Q_W), tk(SW_KV_W),
            pl.BlockSpec((1, TM // LANES, SW_KV_HEADS * V_EXT, LANES), lambda b, t: (b, t, 0, 0)),
            fm(MLA_W), tk(MLA_W),
            pl.BlockSpec((1, 1, MLA_HEADS * V_EXT, TM), lambda b, t: (b, t, 0, 0)),
        ],
        out_shape=[
            sds((B, NA_W, Lc), bf16), sds((B, Lc, NA_W), bf16), sds((B, T, NA_HEADS * V_EXT, TM), bf16),
            sds((B, SW_Q_W, Lc), bf16), sds((B, Lc, SW_KV_W), bf16),
            sds((B, Lc // LANES, SW_KV_HEADS * V_EXT, LANES), bf16),
            sds((B, MLA_W, Lc), bf16), sds((B, Lc, MLA_W), bf16),
            sds((B, T, MLA_HEADS * V_EXT, TM), bf16),
        ],
        compiler_params=_cparams("parallel", "parallel"),
        name="proj",
    )(c_src, x_src, a_row, s_row, win_t, wuq_t, wukv_t, gains, rope_sw, rope_ml)


def _na_kernel(q_ref, kc_ref, k0_ref, k1_ref, k2_ref, vc_ref, v0_ref, v1_ref, v2_ref, bias_ref, o_ref, acc_ref):
    tq = q_ref.shape[2]
    zeros = jnp.zeros((HEAD_DIM, tq), bf16)
    k_refs = (kc_ref, k0_ref, k1_ref, k2_ref)
    v_refs = (vc_ref, v0_ref, v1_ref, v2_ref)

    def scores(h):
        pair = slice((h // 2) * LANES, (h // 2 + 1) * LANES)
        qh = q_ref[0, h * HEAD_DIM:(h + 1) * HEAD_DIM, :]
        qpad = jnp.concatenate([qh, zeros] if h % 2 == 0 else [zeros, qh], axis=0)
        s = [_dot(kr[0, :, pair], qpad) for kr in k_refs]
        for i in range(3):
            s[i + 1] = s[i + 1] + bias_ref[0, h, i * TM:(i + 1) * TM, :]
        return s

    def softmax(h, s):
        m = functools.reduce(jnp.maximum, [jnp.max(x, axis=0, keepdims=True) for x in s])
        return [jnp.exp2((x - m).astype(bf16)) for x in s]

    def weighted_values(h, p):
        vrows = slice(h * V_EXT, (h + 1) * V_EXT)
        o = functools.reduce(jnp.add, [_dot(vr[0, 0, vrows, :], x) for vr, x in zip(v_refs, p)])
        acc_ref[h * HEAD_DIM:(h + 1) * HEAD_DIM, :] = o[:HEAD_DIM] / o[HEAD_DIM:HEAD_DIM + 1]

    _pipelined_heads(NA_HEADS, scores, softmax, weighted_values, *LOCAL_PIPE)
    o_ref[0] = acc_ref[...].T.astype(bf16)


def _na(q_t, k, v_c, bias, off):
    B, _, Lc = q_t.shape
    T = Lc // TM
    nq = T - off
    nblk = T - 1

    def base(i):
        return 1 + jnp.clip(i + off - 2, 0, nblk - 3)

    def btype(i):
        qt = i + off
        return jnp.where(qt == 0, 3, jnp.where(qt == 1, 0, jnp.where(qt == T - 1, 2, 1)))

    kspec = lambda j: pl.BlockSpec((1, TM, NA_W), lambda b, i: (b, base(i) + j, 0))
    vrows = NA_HEADS * V_EXT
    vspec = lambda j: pl.BlockSpec((1, 1, vrows, TM), lambda b, i: (b, base(i) + j, 0, 0))
    return pl.pallas_call(
        _na_kernel,
        grid=(B, nq),
        in_specs=[
            pl.BlockSpec((1, NA_W, TM), lambda b, i: (b, 0, i + off)),
            pl.BlockSpec((1, TM, NA_W), lambda b, i: (b, 0, 0)), kspec(0), kspec(1), kspec(2),
            pl.BlockSpec((1, 1, vrows, TM), lambda b, i: (b, 0, 0, 0)), vspec(0), vspec(1), vspec(2),
            pl.BlockSpec((1, NA_HEADS, NA_KEYS, TM), lambda b, i: (btype(i), 0, 0, 0)),
        ],
        out_specs=pl.BlockSpec((1, TM, BRANCH_W), lambda b, i: (b, i, 0)),
        out_shape=jax.ShapeDtypeStruct((B, nq * TM, BRANCH_W), bf16),
        scratch_shapes=[pltpu.VMEM((BRANCH_W, TM), f32)],
        compiler_params=_cparams("parallel", "arbitrary"),
        name="na_attn",
    )(q_t, k, k, k, k, v_c, v_c, v_c, v_c, bias)


def _na_bias_table(rpb, rows):
    col = jnp.arange(GRID_W)
    dc = jnp.clip(col[:, None] - col[None, :], -(NA_WIN_COLS - 1), NA_WIN_COLS - 1) + (NA_WIN_COLS - 1)
    onehot = (dc[None] == jnp.arange(2 * NA_WIN_COLS - 1)[:, None, None]).astype(f32)
    tile = jnp.einsum("hrd,dkq->hrkq", rpb.astype(f32) * LOG2E, onehot, precision=lax.Precision.HIGHEST)
    c0 = jnp.clip(col - NA_WIN_COLS // 2, 0, GRID_W - NA_WIN_COLS)
    in_win = (col[:, None] >= c0[None, :]) & (col[:, None] < c0[None, :] + NA_WIN_COLS)
    tile = jnp.where(in_win, tile, NEG_INF)
    masked = jnp.full((rpb.shape[0], GRID_W, GRID_W), NEG_INF, f32)
    krows = NA_KEYS // GRID_W

    def one(r0, ks):
        out = []
        for kr in range(ks, ks + krows):
            parts = []
            for qr in range(r0, r0 + NA_QROWS):
                start = min(max(qr - NA_WIN_ROWS // 2, 0), rows - NA_WIN_ROWS)
                parts.append(tile[:, kr - qr + NA_WIN_ROWS - 1] if start <= kr < start + NA_WIN_ROWS else masked)
            out.append(jnp.concatenate(parts, axis=2))
        return jnp.concatenate(out, axis=1)

    tabs = [one(0, 0), one(NA_QROWS, 0), one(rows - NA_QROWS, rows - krows)]
    tabs.append(jnp.full_like(tabs[0], NEG_INF))
    return jnp.stack(tabs)


def _sw_kernel(q_ref, kc_ref, k0_ref, k1_ref, k2_ref, k3_ref, vc_ref, v0_ref, v1_ref, v2_ref, v3_ref,
               sink_ref, o_ref, acc_ref, *, off, ctx_len, total_len):
    tq = q_ref.shape[2]
    qt = pl.program_id(1) + off
    group = SW_Q_HEADS // SW_KV_HEADS
    k_refs = (k0_ref, k1_ref, k2_ref, k3_ref)
    v_refs = (v0_ref, v1_ref, v2_ref, v3_ref)
    kk = lax.broadcasted_iota(jnp.int32, (LANES, tq), 0)
    qpos = qt * tq + lax.broadcasted_iota(jnp.int32, (LANES, tq), 1)
    madd = []
    for i in range(4):
        kpos = (qt * (tq // LANES) - 1 + i) * LANES + kk
        ok = (kpos >= ctx_len) & (kpos < total_len) & (jnp.abs(kpos - qpos) <= SW_WINDOW) & (qt > 0)
        madd.append(jnp.where(ok, 0.0, NEG_INF).astype(f32))
    zeros = jnp.zeros((HEAD_DIM, tq), bf16)

    def scores(h):
        qh = q_ref[0, h * HEAD_DIM:(h + 1) * HEAD_DIM, :]
        qpad = jnp.concatenate([qh, zeros] if h // group == 0 else [zeros, qh], axis=0)
        return [_dot(kc_ref[0], qpad)] + [_dot(kr[0], qpad) + madd[i] for i, kr in enumerate(k_refs)]

    def softmax(h, s):
        sink = sink_ref[h]
        m = functools.reduce(jnp.maximum, [jnp.max(x, axis=0, keepdims=True) for x in s] + [sink])
        return [jnp.exp2((x - m).astype(bf16)) for x in s], jnp.exp2(sink - m)

    def weighted_values(h, item):
        p, p_sink = item
        kv_rows = slice((h // group) * V_EXT, (h // group + 1) * V_EXT)
        parts = [_dot(vc_ref[0, j, kv_rows, :], p[0][j * LANES:(j + 1) * LANES]) for j in range(tq // LANES)]
        parts += [_dot(vr[0, 0, kv_rows, :], x) for vr, x in zip(v_refs, p[1:])]
        o = functools.reduce(jnp.add, parts)
        acc_ref[h * HEAD_DIM:(h + 1) * HEAD_DIM, :] = o[:HEAD_DIM] / (o[HEAD_DIM:HEAD_DIM + 1] + p_sink)

    _pipelined_heads(SW_Q_HEADS, scores, softmax, weighted_values, *LOCAL_PIPE)
    o_ref[0] = acc_ref[...].T.astype(bf16)


def _sw(q_t, k, v_c, sink_rows, off, ctx_len):
    B, _, Lc = q_t.shape
    T = Lc // TM
    nq = T - off
    per = TM // LANES
    lo, hi = ctx_len // LANES, Lc // LANES - 1

    def blk(i, j):
        return jnp.clip((i + off) * per - 1 + j, lo, hi)

    kspec = lambda j: pl.BlockSpec((1, LANES, SW_KV_W), lambda b, i: (b, blk(i, j), 0))
    vrows = SW_KV_HEADS * V_EXT
    vspec = lambda j: pl.BlockSpec((1, 1, vrows, LANES), lambda b, i: (b, blk(i, j), 0, 0))
    kern = functools.partial(_sw_kernel, off=off, ctx_len=ctx_len, total_len=Lc)
    return pl.pallas_call(
        kern,
        grid=(B, nq),
        in_specs=[
            pl.BlockSpec((1, SW_Q_W, TM), lambda b, i: (b, 0, i + off)),
            pl.BlockSpec((1, ctx_len, SW_KV_W), lambda b, i: (b, 0, 0)),
            kspec(0), kspec(1), kspec(2), kspec(3),
            pl.BlockSpec((1, ctx_len // LANES, vrows, LANES), lambda b, i: (b, 0, 0, 0)),
            vspec(0), vspec(1), vspec(2), vspec(3),
            pl.BlockSpec(sink_rows.shape, lambda b, i: (0, 0, 0)),
        ],
        out_specs=pl.BlockSpec((1, TM, BRANCH_W), lambda b, i: (b, i, 0)),
        out_shape=jax.ShapeDtypeStruct((B, nq * TM, BRANCH_W), bf16),
        scratch_shapes=[pltpu.VMEM((BRANCH_W, TM), f32)],
        compiler_params=_cparams("parallel", "arbitrary"),
        name="sw_attn",
    )(q_t, k, k, k, k, k, v_c, v_c, v_c, v_c, v_c, sink_rows)


def _mla_kernel(q_ref, k_ref, v_ref, o_ref, acc_ref, out_ref, *, off):
    nchunk = v_ref.shape[1]
    nstep = (nchunk - 1) // MLA_STEP_CHUNKS
    qt = pl.program_id(1) + off
    nk = jnp.where(qt == 0, 0, nstep)
    tq = q_ref.shape[2]
    acc_ref[...] = jnp.zeros_like(acc_ref)

    def step(ms, r0, chunks):
        nrows = len(chunks) * TM

        def scores(h):
            pair = slice((h // 2) * 2 * MLA_PAD, (h // 2 + 1) * 2 * MLA_PAD)
            qh = q_ref[0, h * MLA_PAD:(h + 1) * MLA_PAD, :]
            zeros = jnp.zeros_like(qh)
            qpad = jnp.concatenate([qh, zeros] if h % 2 == 0 else [zeros, qh], axis=0)
            return [_dot(k_ref[0, pl.ds(r0 + i * TM, TM), pair], qpad) for i in range(len(chunks))]

        ms_new = [None] * MLA_HEADS

        def softmax(h, s):
            m_chunk = functools.reduce(jnp.maximum, [jnp.max(x, axis=0, keepdims=True) for x in s])
            ms_new[h] = jnp.maximum(ms[h], m_chunk)
            return jnp.exp2(ms[h] - ms_new[h]), [jnp.exp2(x - ms_new[h]).astype(bf16) for x in s]

        def weighted_values(h, item):
            alpha, p = item
            rows = slice(h * V_EXT, (h + 1) * V_EXT)
            upd = functools.reduce(jnp.add, [_dot(v_ref[0, c, rows, :], x) for c, x in zip(chunks, p)])
            acc_ref[rows, :] = alpha * acc_ref[rows, :] + upd

        _pipelined_heads(MLA_HEADS, scores, softmax, weighted_values, *MLA_PIPE)
        return tuple(ms_new)

    ms = step(tuple(jnp.full((1, tq), NEG_INF, f32) for _ in range(MLA_HEADS)), 0, [0])

    def body(i, ms):
        c0 = 1 + i * MLA_STEP_CHUNKS
        return step(ms, pl.multiple_of(c0 * TM, TM), [c0 + j for j in range(MLA_STEP_CHUNKS)])

    lax.fori_loop(0, nk, body, ms)
    for h in range(MLA_HEADS):
        num = acc_ref[h * V_EXT:h * V_EXT + MLA_V, :]
        out_ref[h * MLA_V:(h + 1) * MLA_V, :] = num / acc_ref[h * V_EXT + MLA_V:h * V_EXT + MLA_V + 1, :]
    o_ref[0] = out_ref[...].T.astype(bf16)


def _mla(q_t, k, v_c, off):
    B, _, Lc = q_t.shape
    T = Lc // TM
    nq = T - off
    return pl.pallas_call(
        functools.partial(_mla_kernel, off=off),
        grid=(B, nq),
        in_specs=[
            pl.BlockSpec((1, MLA_HEADS * MLA_PAD, TM), lambda b, i: (b, 0, i + off)),
            pl.BlockSpec((1, Lc, MLA_HEADS * MLA_PAD), lambda b, i: (b, 0, 0)),
            pl.BlockSpec((1, T, MLA_HEADS * V_EXT, TM), lambda b, i: (b, 0, 0, 0)),
        ],
        out_specs=pl.BlockSpec((1, TM, BRANCH_W), lambda b, i: (b, i, 0)),
        out_shape=jax.ShapeDtypeStruct((B, nq * TM, BRANCH_W), bf16),
        scratch_shapes=[pltpu.VMEM((MLA_HEADS * V_EXT, TM), f32), pltpu.VMEM((BRANCH_W, TM), f32)],
        compiler_params=_cparams("parallel", "arbitrary"),
        name="mla_attn",
    )(q_t, k, v_c)


def _merge_kernel(c_ref, x_ref, a_ref, s_ref, ga_ref, ona_ref, osw_ref, oml_ref, wg_ref, wb_ref, wo_ref, o_ref,
                  *, off):
    x = _stream_tile(c_ref, x_ref, pl.program_id(1) + off)
    d = x.shape[1]
    hb = _norm_mod(x, a_ref[0], s_ref[0]).astype(bf16)
    z = None
    for n, o_n in enumerate((ona_ref, osw_ref, oml_ref)):
        g = jax.nn.sigmoid(_dot(hb, wg_ref[:, n * d:(n + 1) * d]))
        y = g * _dot(o_n[0], wb_ref[n])
        z = y if z is None else z + y
    o_ref[0] = x + ga_ref[0] * _dot(z.astype(bf16), wo_ref[...])


def _merge(c_src, x_src, a_row, s_row, g_row, o_na, o_sw, o_ml, w_gate, w_branch, w_out, off):
    B, _, D = x_src.shape
    nt = x_src.shape[1] // TM + (0 if c_src is x_src else 1) - off
    nb = a_row.shape[0] - 1
    row = lambda b, i: (jnp.where(i + off == 0, nb, b), 0, 0)
    ospec = pl.BlockSpec((1, TM, BRANCH_W), lambda b, i: (b, i, 0))
    return pl.pallas_call(
        functools.partial(_merge_kernel, off=off),
        grid=(B, nt),
        in_specs=[
            *_stream_specs(c_src, x_src, off),
            pl.BlockSpec((1, 1, D), row), pl.BlockSpec((1, 1, D), row), pl.BlockSpec((1, 1, D), row),
            ospec, ospec, ospec,
            _const_spec(w_gate.shape), _const_spec(w_branch.shape), _const_spec(w_out.shape),
        ],
        out_specs=pl.BlockSpec((1, TM, D), lambda b, i: (b, i, 0)),
        out_shape=jax.ShapeDtypeStruct((B, nt * TM, D), f32),
        compiler_params=_cparams("parallel", "parallel"),
        name="merge",
    )(c_src, x_src, a_row, s_row, g_row, o_na, o_sw, o_ml, w_gate, w_branch, w_out)


def _ffn_kernel(x_ref, xp_ref, xn_ref, a_ref, s_ref, gf_ref, wup_ref, cw_ref, cb_ref, wdn_ref, o_ref, u_ref,
                *, ctx_tiles):
    tm = x_ref.shape[1]
    halo = xp_ref.shape[1]
    nc = wup_ref.shape[0]
    t = pl.program_id(1)
    nt = pl.num_programs(1)
    x = x_ref[0]
    xe = jnp.concatenate([xp_ref[0], x, xn_ref[0]], axis=0)
    hb = _norm_mod(xe, a_ref[0], s_ref[0]).astype(bf16)
    left_ok = jnp.where(jnp.logical_and(t != 0, t != ctx_tiles), 1.0, 0.0)
    right_ok = jnp.where(jnp.logical_and(t != nt - 1, t != ctx_tiles - 1), 1.0, 0.0)

    def up(c):
        u = _dot(hb, wup_ref[c])
        slot = u_ref.at[c % 2]
        slot[:halo] = u[:halo] * left_ok
        slot[halo:halo + tm] = u[halo:halo + tm]
        slot[halo + tm:] = u[halo + tm:] * right_ok

    def conv_act(c):
        slot = u_ref.at[c % 2]
        w = cw_ref[c]
        uc = (w[0:1] * slot[halo - 1:halo - 1 + tm] + w[1:2] * slot[halo:halo + tm]
              + w[2:3] * slot[halo + 1:halo + 1 + tm] + cb_ref[c])
        g, v = uc[:, :FF_CHUNK], uc[:, FF_CHUNK:]
        return (g * jax.nn.sigmoid(g) * v).astype(bf16)

    acc = jnp.zeros(x.shape, f32)
    pending = []
    up(0)
    for c in range(nc):
        if c + 1 < nc:
            up(c + 1)
        pending.append((c, conv_act(c)))
        if len(pending) > FF_DOWN_BEHIND:
            j, act = pending.pop(0)
            acc = acc + _dot(act, wdn_ref[j])
    for j, act in pending:
        acc = acc + _dot(act, wdn_ref[j])
    o_ref[0] = x + gf_ref[0] * acc


def _ffn(x1, a_row, s_row, g_row, w_up_c, conv_w_c, conv_b_c, w_down_c, ctx_tiles):
    B, Lt, D = x1.shape
    nt = Lt // TM
    nb = a_row.shape[0] - 1
    per = TM // SUBLANES
    last = Lt // SUBLANES - 1
    row = lambda b, t: (jnp.where(t < ctx_tiles, nb, b), 0, 0)
    return pl.pallas_call(
        functools.partial(_ffn_kernel, ctx_tiles=ctx_tiles),
        grid=(B, nt),
        in_specs=[
            pl.BlockSpec((1, TM, D), lambda b, t: (b, t, 0)),
            pl.BlockSpec((1, SUBLANES, D), lambda b, t: (b, jnp.maximum(t * per - 1, 0), 0)),
            pl.BlockSpec((1, SUBLANES, D), lambda b, t: (b, jnp.minimum((t + 1) * per, last), 0)),
            pl.BlockSpec((1, 1, D), row), pl.BlockSpec((1, 1, D), row), pl.BlockSpec((1, 1, D), row),
            _const_spec(w_up_c.shape), _const_spec(conv_w_c.shape), _const_spec(conv_b_c.shape),
            _const_spec(w_down_c.shape),
        ],
        out_specs=pl.BlockSpec((1, TM, D), lambda b, t: (b, t, 0)),
        out_shape=jax.ShapeDtypeStruct((B, Lt, D), f32),
        scratch_shapes=[pltpu.VMEM((2, TM + 2 * SUBLANES, 2 * FF_CHUNK), f32)],
        compiler_params=_cparams("parallel", "arbitrary"),
        name="ffn",
    )(x1, x1, x1, a_row, s_row, g_row, w_up_c, conv_w_c, conv_b_c, w_down_c)


def _rope_table(seq, ctx_len, n):
    t = jnp.arange(seq)
    inv = ROPE_BASE ** (-jnp.arange(n, dtype=f32) / n)
    ar = (t // GRID_W).astype(f32)[None, :] * inv[:, None]
    ac = (t % GRID_W).astype(f32)[None, :] * inv[:, None]
    lat = jnp.concatenate([jnp.cos(ar), jnp.sin(ar), jnp.cos(ac), jnp.sin(ac)], axis=0)
    one, zero = jnp.ones((n, ctx_len), f32), jnp.zeros((n, ctx_len), f32)
    return jnp.concatenate([jnp.concatenate([one, zero, one, zero], axis=0), lat], axis=1)


def _chunked_ffn_weights(w_up, conv_w, conv_b, w_down):
    d_ff = w_down.shape[0]
    nc = d_ff // FF_CHUNK
    pair = lambda a: jnp.concatenate([a[..., :d_ff].reshape(a.shape[:-1] + (nc, FF_CHUNK)),
                                      a[..., d_ff:].reshape(a.shape[:-1] + (nc, FF_CHUNK))], axis=-1)
    w_up_c = jnp.moveaxis(pair(w_up), 1, 0).astype(bf16)
    conv_w_c = jnp.moveaxis(pair(conv_w), 1, 0)
    conv_b_c = pair(conv_b)[:, None, :]
    w_down_c = w_down.reshape(nc, FF_CHUNK, w_down.shape[1]).astype(bf16)
    return w_up_c, conv_w_c, conv_b_c, w_down_c


def kernel(x, c, ctx, c_ctx, w_ada, b_ada, g_mix, g_ffn, w_in, na_q_norm, na_k_norm, na_rpb, sw_q_norm, sw_k_norm,
           sw_sink, mla_q_rank_norm, mla_kv_rank_norm, w_uq, w_ukv, mla_q_norm, mla_k_norm, w_branch, w_out,
           w_up, conv_w, conv_b, w_down):
    B, S, D = x.shape
    C = ctx.shape[1]
    depth = w_ada.shape[0]
    assert C == TM and S % TM == 0 and GRID_W * NA_QROWS == TM and (S // GRID_W) * GRID_W == S
    assert S // GRID_W >= NA_KEYS // GRID_W and w_down.shape[1] % FF_CHUNK == 0
    assert (S // TM) % MLA_STEP_CHUNKS == 0
    rows = S // GRID_W

    c_src, x_src = ctx, x
    n_mod = -(-(B + 1) // SUBLANES) * SUBLANES
    c_all = jnp.zeros((n_mod, D), f32).at[:B].set(c).at[B].set(c_ctx)
    zero = jnp.zeros_like(g_mix)
    g_sel = jnp.stack([zero, g_mix, zero, zero, g_ffn, zero], axis=1)[:, :, None, :]
    mod = _ada(c_all, w_ada, b_ada, g_sel)

    rope_sw = _rope_table(S, C, HEAD_DIM // 4)
    rope_ml = _rope_table(S, C, MLA_ROPE // 4)
    group = SW_Q_HEADS // SW_KV_HEADS

    for l in range(depth):
        off = 0 if l < depth - 1 else 1
        m6 = mod[l, :B + 1].reshape(B + 1, 6, 1, D)
        shift_a, a_mix, gate_a, shift_f, a_ffn, gate_f = (m6[:, j] for j in range(6))
        w_in_t = w_in[l][:, :N_STREAM].T.astype(bf16)
        gains = jnp.concatenate([na_q_norm[l], na_k_norm[l], sw_q_norm[l], sw_k_norm[l], mla_q_rank_norm[l],
                                 mla_kv_rank_norm[l], mla_q_norm[l], mla_k_norm[l]])[:, None]
        (na_q, na_k, na_v, sw_q, sw_k, sw_v, m_q, m_k, m_v) = _proj(
            c_src, x_src, a_mix, shift_a, w_in_t, w_uq[l].T.astype(bf16), w_ukv[l].T.astype(bf16),
            gains, rope_sw, rope_ml)
        o_na = _na(na_q, na_k, na_v, _na_bias_table(na_rpb[l], rows), off)
        sink_rows = jnp.broadcast_to((sw_sink[l] * LOG2E)[:, None, None], (SW_Q_HEADS, 1, TM))
        o_sw = _sw(sw_q, sw_k, sw_v, sink_rows, off, C)
        o_ml = _mla(m_q, m_k, m_v, off)
        x1 = _merge(c_src, x_src, a_mix, shift_a, gate_a, o_na, o_sw, o_ml, w_in[l][:, N_STREAM:].astype(bf16),
                    w_branch[l].astype(bf16), w_out[l].astype(bf16), off)
        xc = _ffn(x1, a_ffn, shift_f, gate_f, *_chunked_ffn_weights(w_up[l], conv_w[l], conv_b[l], w_down[l]),
                  ctx_tiles=1 - off)
        c_src = x_src = xc
    return xc
```

```python
import functools

import jax
import jax.numpy as jnp
from jax import lax
from jax.experimental import pallas as pl
from jax.experimental.pallas import tpu as pltpu

f32 = jnp.float32
bf16 = jnp.bfloat16

GRID_W = 64
HEAD_DIM = 64
ROPE_BASE = 10000.0
NORM_EPS = 1e-6
NEG_INF = -1e30
NA_HEADS = 8
NA_WIN_ROWS = 8
NA_WIN_COLS = 16
SW_Q_HEADS = 8
SW_KV_HEADS = 2
SW_WINDOW = 128
MLA_HEADS = 8
MLA_Q_RANK = 384
MLA_KV_RANK = 256
MLA_NOPE = 64
MLA_ROPE = 32
MLA_V = 64
N_BRANCH = 3
BRANCH_W = 512
CONV_W = 3

LANES = 128
SUBLANES = 8
TM = 256
MLA_PAD = 128
LOCAL_PIPE = (2, 1)
MLA_PIPE = (4, 3)
FF_CHUNK = 256
FF_DOWN_BEHIND = 2
NA_QROWS = TM // GRID_W
NA_KEYS = 3 * TM
BF16_ROWS = 16
V_EXT = HEAD_DIM + BF16_ROWS
LOG2E = 1.4426950408889634
VMEM_LIMIT = 56 * 1024 * 1024

_SEG = {}
_o = 0
NA_W = NA_HEADS * HEAD_DIM
SW_Q_W = SW_Q_HEADS * HEAD_DIM
SW_KV_W = SW_KV_HEADS * HEAD_DIM
MLA_W = MLA_HEADS * MLA_PAD
for _name, _n in (("na_q", NA_W), ("na_k", NA_W), ("na_v", NA_W), ("sw_q", SW_Q_W), ("sw_k", SW_KV_W),
                  ("sw_v", SW_KV_W), ("c_q", MLA_Q_RANK), ("c_kv", MLA_KV_RANK), ("k_r", MLA_ROPE)):
    _SEG[_name] = (_o, _o + _n)
    _o += _n
N_STREAM = _o
_GN = {}
_o = 0
for _name, _n in (("na_q", 64), ("na_k", 64), ("sw_q", 64), ("sw_k", 64), ("mla_qr", MLA_Q_RANK),
                  ("mla_kvr", MLA_KV_RANK), ("mla_q", 96), ("mla_k", 96)):
    _GN[_name] = (_o, _o + _n)
    _o += _n
N_GAIN = _o


def _dot(a, b):
    return jnp.dot(a, b, preferred_element_type=f32)


def _cparams(*sem):
    return pltpu.CompilerParams(dimension_semantics=sem, vmem_limit_bytes=VMEM_LIMIT)


def _const_spec(shape):
    nd = len(shape)
    return pl.BlockSpec(shape, lambda *_: (0,) * nd, pipeline_mode=pl.Buffered(1))


def _ada_kernel(c_ref, w_ref, b_ref, g_ref, o_ref):
    n = pl.program_id(1)
    c = c_ref[...]
    sc = (c * jax.nn.sigmoid(c)).astype(bf16)
    mod = _dot(sc, w_ref[0].astype(bf16)) + b_ref[0]
    fold = jnp.logical_or(n == 1, n == 4)
    o_ref[0] = jnp.where(fold, g_ref[0, 0] * (1.0 + mod), mod)


def _ada(c_all, w_ada, b_ada, g_sel):
    L, D, _ = w_ada.shape
    R = c_all.shape[0]
    return pl.pallas_call(
        _ada_kernel,
        grid=(L, 6),
        in_specs=[
            pl.BlockSpec((R, D), lambda l, n: (0, 0)),
            pl.BlockSpec((1, D, D), lambda l, n: (l, 0, n)),
            pl.BlockSpec((1, 1, D), lambda l, n: (l, 0, n)),
            pl.BlockSpec((1, 1, 1, D), lambda l, n: (l, n, 0, 0)),
        ],
        out_specs=pl.BlockSpec((1, R, D), lambda l, n: (l, 0, n)),
        out_shape=jax.ShapeDtypeStruct((L, R, 6 * D), f32),
        compiler_params=_cparams("parallel", "parallel"),
        name="ada",
    )(c_all, w_ada, b_ada.reshape(L, 1, 6 * D), g_sel)


def _norm_mod(x, a, s):
    ms = jnp.mean(x * x, axis=-1, keepdims=True)
    return (x * lax.rsqrt(ms + NORM_EPS)) * a + s


def _head_norm(y3, gain):
    ms = jnp.mean(y3 * y3, axis=1, keepdims=True)
    return y3 * lax.rsqrt(ms + NORM_EPS) * gain[None]


def _rope(t3, tab, lo, n):
    cr, sr, cc, sc = (tab[i * n:(i + 1) * n][None] for i in range(4))
    a, b, c, e = (t3[:, lo + i * n:lo + (i + 1) * n] for i in range(4))
    return [a * cr - b * sr, a * sr + b * cr, c * cc - e * sc, c * sc + e * cc]


def _pipelined_heads(n, scores, softmax, weighted_values, ahead, behind):
    s, pending = {}, []
    for h in range(min(ahead, n)):
        s[h] = scores(h)
    for h in range(n):
        if h + ahead < n:
            s[h + ahead] = scores(h + ahead)
        pending.append((h, softmax(h, s.pop(h))))
        if len(pending) > behind:
            weighted_values(*pending.pop(0))
    for item in pending:
        weighted_values(*item)


def _stream_tile(c_ref, x_ref, t):
    return jnp.where(t == 0, c_ref[0], x_ref[0])


def _stream_specs(c_src, x_src, off):
    shift = 0 if c_src is x_src else 1
    d = x_src.shape[2]
    return [pl.BlockSpec((1, TM, d), lambda b, i: (b, 0, 0)),
            pl.BlockSpec((1, TM, d), lambda b, i: (b, jnp.maximum(i + off - shift, 0), 0))]


def _proj_kernel(c_ref, x_ref, a_ref, s_ref, win_ref, wuq_ref, wukv_ref, gn_ref, rsw_ref, rml_ref,
                 naq_ref, nak_ref, nav_ref, swq_ref, swk_ref, swv_ref, mq_ref, mk_ref, mv_ref):
    tm = x_ref.shape[1]
    h = _norm_mod(_stream_tile(c_ref, x_ref, pl.program_id(1)), a_ref[0], s_ref[0])
    ht = h.T.astype(bf16)

    def seg(name):
        lo, hi = _SEG[name]
        return _dot(win_ref[lo:hi, :], ht)

    def gain(name):
        lo, hi = _GN[name]
        return gn_ref[lo:hi, :]

    def with_ones(v, heads):
        v3 = v.reshape(heads, HEAD_DIM, tm)
        ones = jnp.ones((heads, BF16_ROWS, tm), f32)
        return jnp.concatenate([v3, ones], axis=1).reshape(heads * V_EXT, tm).astype(bf16)

    def rank_norm(y, name):
        return (y * lax.rsqrt(jnp.mean(y * y, axis=0, keepdims=True) + NORM_EPS) * gain(name)).astype(bf16)

    y_cq, y_ckv, kr = seg("c_q"), seg("c_kv"), seg("k_r")
    y_naq, y_nak = seg("na_q"), seg("na_k")
    mq = _dot(wuq_ref[...], rank_norm(y_cq, "mla_qr"))
    kv = _dot(wukv_ref[...], rank_norm(y_ckv, "mla_kvr"))
    y_nav, y_swq, y_swk, y_swv = seg("na_v"), seg("sw_q"), seg("sw_k"), seg("sw_v")

    qs = HEAD_DIM ** -0.5 * LOG2E
    q = _head_norm(y_naq.reshape(NA_HEADS, HEAD_DIM, tm), gain("na_q")) * qs
    naq_ref[0] = q.reshape(NA_HEADS * HEAD_DIM, tm).astype(bf16)
    k = _head_norm(y_nak.reshape(NA_HEADS, HEAD_DIM, tm), gain("na_k"))
    nak_ref[0] = k.reshape(NA_HEADS * HEAD_DIM, tm).T.astype(bf16)
    nav_ref[0, 0] = with_ones(y_nav, NA_HEADS)

    rml = rml_ref[...]
    nr = MLA_ROPE // 4
    dq = MLA_NOPE + MLA_ROPE
    pad = jnp.zeros((MLA_HEADS, MLA_PAD - dq, tm), f32)
    mq = _head_norm(mq.reshape(MLA_HEADS, dq, tm), gain("mla_q"))
    mq = jnp.concatenate([mq[:, :MLA_NOPE]] + _rope(mq, rml, MLA_NOPE, nr), axis=1) * (dq ** -0.5 * LOG2E)
    mq_ref[0] = jnp.concatenate([mq, pad], axis=1).reshape(MLA_HEADS * MLA_PAD, tm).astype(bf16)

    kv = kv.reshape(MLA_HEADS, MLA_NOPE + MLA_V, tm)
    kn, v = kv[:, :MLA_NOPE], kv[:, MLA_NOPE:]
    ss = jnp.sum(kn * kn, axis=1, keepdims=True) + jnp.sum(kr * kr, axis=0, keepdims=True)[None]
    r = lax.rsqrt(ss / dq + NORM_EPS)
    gk = gain("mla_k")
    kn = kn * r * gk[:MLA_NOPE][None]
    krh = kr[None] * r * gk[MLA_NOPE:][None]
    mk = jnp.concatenate([kn] + _rope(krh, rml, 0, nr) + [pad], axis=1)
    mk_ref[0] = mk.reshape(MLA_HEADS * MLA_PAD, tm).T.astype(bf16)
    mv_ref[0, 0] = with_ones(v.reshape(MLA_HEADS * MLA_V, tm), MLA_HEADS)

    rsw = rsw_ref[...]
    nq = HEAD_DIM // 4
    q = _head_norm(y_swq.reshape(SW_Q_HEADS, HEAD_DIM, tm), gain("sw_q"))
    q = jnp.concatenate(_rope(q, rsw, 0, nq), axis=1) * qs
    swq_ref[0] = q.reshape(SW_Q_HEADS * HEAD_DIM, tm).astype(bf16)
    k = _head_norm(y_swk.reshape(SW_KV_HEADS, HEAD_DIM, tm), gain("sw_k"))
    k = jnp.concatenate(_rope(k, rsw, 0, nq), axis=1)
    swk_ref[0] = k.reshape(SW_KV_HEADS * HEAD_DIM, tm).T.astype(bf16)
    v = with_ones(y_swv, SW_KV_HEADS)
    for j in range(tm // LANES):
        swv_ref[0, j] = v[:, j * LANES:(j + 1) * LANES]


def _proj(c_src, x_src, a_row, s_row, win_t, wuq_t, wukv_t, gains, rope_sw, rope_ml):
    B, _, D = x_src.shape
    T = x_src.shape[1] // TM + (0 if c_src is x_src else 1)
    Lc = T * TM
    nb = a_row.shape[0] - 1
    row = lambda b, t: (jnp.where(t == 0, nb, b), 0, 0)
    fm = lambda n: pl.BlockSpec((1, n, TM), lambda b, t: (b, 0, t))
    tk = lambda n: pl.BlockSpec((1, TM, n), lambda b, t: (b, t, 0))
    sds = jax.ShapeDtypeStruct
    return pl.pallas_call(
        _proj_kernel,
        grid=(B, T),
        in_specs=[
            *_stream_specs(c_src, x_src, 0),
            pl.BlockSpec((1, 1, D), row),
            pl.BlockSpec((1, 1, D), row),
            _const_spec(win_t.shape), _const_spec(wuq_t.shape), _const_spec(wukv_t.shape),
            _const_spec(gains.shape),
            pl.BlockSpec((rope_sw.shape[0], TM), lambda b, t: (0, t)),
            pl.BlockSpec((rope_ml.shape[0], TM), lambda b, t: (0, t)),
        ],
        out_specs=[
            fm(NA_W), tk(NA_W), pl.BlockSpec((1, 1, NA_HEADS * V_EXT, TM), lambda b, t: (b, t, 0, 0)),
            fm(SW_Q_W), tk(SW_KV_W),
            pl.BlockSpec((1, TM // LANES, SW_KV_HEADS * V_EXT, LANES), lambda b, t: (b, t, 0, 0)),
            fm(MLA_W), tk(MLA_W),
            pl.BlockSpec((1, 1, MLA_HEADS * V_EXT, TM), lambda b, t: (b, t, 0, 0)),
        ],
        out_shape=[
            sds((B, NA_W, Lc), bf16), sds((B, Lc, NA_W), bf16), sds((B, T, NA_HEADS * V_EXT, TM), bf16),
            sds((B, SW_Q_W, Lc), bf16), sds((B, Lc, SW_KV_W), bf16),
            sds((B, Lc // LANES, SW_KV_HEADS * V_EXT, LANES), bf16),
            sds((B, MLA_W, Lc), bf16), sds((B, Lc, MLA_W), bf16),
            sds((B, T, MLA_HEADS * V_EXT, TM), bf16),
        ],
        compiler_params=_cparams("parallel", "parallel"),
        name="proj",
    )(c_src, x_src, a_row, s_row, win_t, wuq_t, wukv_t, gains, rope_sw, rope_ml)


def _na_kernel(q_ref, kc_ref, k0_ref, k1_ref, k2_ref, vc_ref, v0_ref, v1_ref, v2_ref, bias_ref, o_ref, acc_ref):
    tq = q_ref.shape[2]
    zeros = jnp.zeros((HEAD_DIM, tq), bf16)
    k_refs = (kc_ref, k0_ref, k1_ref, k2_ref)
    v_refs = (vc_ref, v0_ref, v1_ref, v2_ref)

    def scores(h):
        pair = slice((h // 2) * LANES, (h // 2 + 1) * LANES)
        qh = q_ref[0, h * HEAD_DIM:(h + 1) * HEAD_DIM, :]
        qpad = jnp.concatenate([qh, zeros] if h % 2 == 0 else [zeros, qh], axis=0)
        s = [_dot(kr[0, :, pair], qpad) for kr in k_refs]
        for i in range(3):
            s[i + 1] = s[i + 1] + bias_ref[0, h, i * TM:(i + 1) * TM, :]
        return s

    def softmax(h, s):
        m = functools.reduce(jnp.maximum, [jnp.max(x, axis=0, keepdims=True) for x in s])
        return [jnp.exp2((x - m).astype(bf16)) for x in s]

    def weighted_values(h, p):
        vrows = slice(h * V_EXT, (h + 1) * V_EXT)
        o = functools.reduce(jnp.add, [_dot(vr[0, 0, vrows, :], x) for vr, x in zip(v_refs, p)])
        acc_ref[h * HEAD_DIM:(h + 1) * HEAD_DIM, :] = o[:HEAD_DIM] / o[HEAD_DIM:HEAD_DIM + 1]

    _pipelined_heads(NA_HEADS, scores, softmax, weighted_values, *LOCAL_PIPE)
    o_ref[0] = acc_ref[...].T.astype(bf16)


def _na(q_t, k, v_c, bias, off):
    B, _, Lc = q_t.shape
    T = Lc // TM
    nq = T - off
    nblk = T - 1

    def base(i):
        return 1 + jnp.clip(i + off - 2, 0, nblk - 3)

    def btype(i):
        qt = i + off
        return jnp.where(qt == 0, 3, jnp.where(qt == 1, 0, jnp.where(qt == T - 1, 2, 1)))

    kspec = lambda j: pl.BlockSpec((1, TM, NA_W), lambda b, i: (b, base(i) + j, 0))
    vrows = NA_HEADS * V_EXT
    vspec = lambda j: pl.BlockSpec((1, 1, vrows, TM), lambda b, i: (b, base(i) + j, 0, 0))
    return pl.pallas_call(
        _na_kernel,
        grid=(B, nq),
        in_specs=[
            pl.BlockSpec((1, NA_W, TM), lambda b, i: (b, 0, i + off)),
            pl.BlockSpec((1, TM, NA_W), lambda b, i: (b, 0, 0)), kspec(0), kspec(1), kspec(2),
            pl.BlockSpec((1, 1, vrows, TM), lambda b, i: (b, 0, 0, 0)), vspec(0), vspec(1), vspec(2),
            pl.BlockSpec((1, NA_HEADS, NA_KEYS, TM), lambda b, i: (btype(i), 0, 0, 0)),
        ],
        out_specs=pl.BlockSpec((1, TM, BRANCH_W), lambda b, i: (b, i, 0)),
        out_shape=jax.ShapeDtypeStruct((B, nq * TM, BRANCH_W), bf16),
        scratch_shapes=[pltpu.VMEM((BRANCH_W, TM), f32)],
        compiler_params=_cparams("parallel", "arbitrary"),
        name="na_attn",
    )(q_t, k, k, k, k, v_c, v_c, v_c, v_c, bias)


def _na_bias_table(rpb, rows):
    col = jnp.arange(GRID_W)
    dc = jnp.clip(col[:, None] - col[None, :], -(NA_WIN_COLS - 1), NA_WIN_COLS - 1) + (NA_WIN_COLS - 1)
    onehot = (dc[None] == jnp.arange(2 * NA_WIN_COLS - 1)[:, None, None]).astype(f32)
    tile = jnp.einsum("hrd,dkq->hrkq", rpb.astype(f32) * LOG2E, onehot, precision=lax.Precision.HIGHEST)
    c0 = jnp.clip(col - NA_WIN_COLS // 2, 0, GRID_W - NA_WIN_COLS)
    in_win = (col[:, None] >= c0[None, :]) & (col[:, None] < c0[None, :] + NA_WIN_COLS)
    tile = jnp.where(in_win, tile, NEG_INF)
    masked = jnp.full((rpb.shape[0], GRID_W, GRID_W), NEG_INF, f32)
    krows = NA_KEYS // GRID_W

    def one(r0, ks):
        out = []
        for kr in range(ks, ks + krows):
            parts = []
            for qr in range(r0, r0 + NA_QROWS):
                start = min(max(qr - NA_WIN_ROWS // 2, 0), rows - NA_WIN_ROWS)
                parts.append(tile[:, kr - qr + NA_WIN_ROWS - 1] if start <= kr < start + NA_WIN_ROWS else masked)
            out.append(jnp.concatenate(parts, axis=2))
        return jnp.concatenate(out, axis=1)

    tabs = [one(0, 0), one(NA_QROWS, 0), one(rows - NA_QROWS, rows - krows)]
    tabs.append(jnp.full_like(tabs[0], NEG_INF))
    return jnp.stack(tabs)


def _sw_kernel(q_ref, kc_ref, k0_ref, k1_ref, k2_ref, k3_ref, vc_ref, v0_ref, v1_ref, v2_ref, v3_ref,
               sink_ref, o_ref, acc_ref, *, off, ctx_len, total_len):
    tq = q_ref.shape[2]
    qt = pl.program_id(1) + off
    group = SW_Q_HEADS // SW_KV_HEADS
    k_refs = (k0_ref, k1_ref, k2_ref, k3_ref)
    v_refs = (v0_ref, v1_ref, v2_ref, v3_ref)
    kk = lax.broadcasted_iota(jnp.int32, (LANES, tq), 0)
    qpos = qt * tq + lax.broadcasted_iota(jnp.int32, (LANES, tq), 1)
    madd = []
    for i in range(4):
        kpos = (qt * (tq // LANES) - 1 + i) * LANES + kk
        ok = (kpos >= ctx_len) & (kpos < total_len) & (jnp.abs(kpos - qpos) <= SW_WINDOW) & (qt > 0)
        madd.append(jnp.where(ok, 0.0, NEG_INF).astype(f32))
    zeros = jnp.zeros((HEAD_DIM, tq), bf16)

    def scores(h):
        qh = q_ref[0, h * HEAD_DIM:(h + 1) * HEAD_DIM, :]
        qpad = jnp.concatenate([qh, zeros] if h // group == 0 else [zeros, qh], axis=0)
        return [_dot(kc_ref[0], qpad)] + [_dot(kr[0], qpad) + madd[i] for i, kr in enumerate(k_refs)]

    def softmax(h, s):
        sink = sink_ref[h]
        m = functools.reduce(jnp.maximum, [jnp.max(x, axis=0, keepdims=True) for x in s] + [sink])
        return [jnp.exp2((x - m).astype(bf16)) for x in s], jnp.exp2(sink - m)

    def weighted_values(h, item):
        p, p_sink = item
        kv_rows = slice((h // group) * V_EXT, (h // group + 1) * V_EXT)
        parts = [_dot(vc_ref[0, j, kv_rows, :], p[0][j * LANES:(j + 1) * LANES]) for j in range(tq // LANES)]
        parts += [_dot(vr[0, 0, kv_rows, :], x) for vr, x in zip(v_refs, p[1:])]
        o = functools.reduce(jnp.add, parts)
        acc_ref[h * HEAD_DIM:(h + 1) * HEAD_DIM, :] = o[:HEAD_DIM] / (o[HEAD_DIM:HEAD_DIM + 1] + p_sink)

    _pipelined_heads(SW_Q_HEADS, scores, softmax, weighted_values, *LOCAL_PIPE)
    o_ref[0] = acc_ref[...].T.astype(bf16)


def _sw(q_t, k, v_c, sink_rows, off, ctx_len):
    B, _, Lc = q_t.shape
    T = Lc // TM
    nq = T - off
    per = TM // LANES
    lo, hi = ctx_len // LANES, Lc // LANES - 1

    def blk(i, j):
        return jnp.clip((i + off) * per - 1 + j, lo, hi)

    kspec = lambda j: pl.BlockSpec((1, LANES, SW_KV_W), lambda b, i: (b, blk(i, j), 0))
    vrows = SW_KV_HEADS * V_EXT
    vspec = lambda j: pl.BlockSpec((1, 1, vrows, LANES), lambda b, i: (b, blk(i, j), 0, 0))
    kern = functools.partial(_sw_kernel, off=off, ctx_len=ctx_len, total_len=Lc)
    return pl.pallas_call(
        kern,
        grid=(B, nq),
        in_specs=[
            pl.BlockSpec((1, SW_Q_W, TM), lambda b, i: (b, 0, i + off)),
            pl.BlockSpec((1, ctx_len, SW_KV_W), lambda b, i: (b, 0, 0)),
            kspec(0), kspec(1), kspec(2), kspec(3),
            pl.BlockSpec((1, ctx_len // LANES, vrows, LANES), lambda b, i: (b, 0, 0, 0)),
            vspec(0), vspec(1), vspec(2), vspec(3),
            pl.BlockSpec(sink_rows.shape, lambda b, i: (0, 0, 0)),
        ],
        out_specs=pl.BlockSpec((1, TM, BRANCH_W), lambda b, i: (b, i, 0)),
        out_shape=jax.ShapeDtypeStruct((B, nq * TM, BRANCH_W), bf16),
        scratch_shapes=[pltpu.VMEM((BRANCH_W, TM), f32)],
        compiler_params=_cparams("parallel", "arbitrary"),
        name="sw_attn",
    )(q_t, k, k, k, k, k, v_c, v_c, v_c, v_c, v_c, sink_rows)


def _mla_kernel(q_ref, k_ref, v_ref, o_ref, acc_ref, out_ref, *, off):
    nchunk = v_ref.shape[1]
    tq = q_ref.shape[2]

    def attend(chunks):
        acc_ref[...] = jnp.zeros_like(acc_ref)
        ms = [jnp.full((1, tq), NEG_INF, f32) for _ in range(MLA_HEADS)]

        def scores(u):
            h, c = u % MLA_HEADS, chunks[u // MLA_HEADS]
            pair = slice((h // 2) * 2 * MLA_PAD, (h // 2 + 1) * 2 * MLA_PAD)
            qh = q_ref[0, h * MLA_PAD:(h + 1) * MLA_PAD, :]
            zeros = jnp.zeros_like(qh)
            qpad = jnp.concatenate([qh, zeros] if h % 2 == 0 else [zeros, qh], axis=0)
            return _dot(k_ref[0, c * TM:(c + 1) * TM, pair], qpad)

        def softmax(u, s):
            h = u % MLA_HEADS
            m_old = ms[h]
            ms[h] = jnp.maximum(m_old, jnp.max(s, axis=0, keepdims=True))
            return jnp.exp2(m_old - ms[h]), jnp.exp2(s - ms[h]).astype(bf16)

        def weighted_values(u, item):
            h, c = u % MLA_HEADS, chunks[u // MLA_HEADS]
            alpha, p = item
            rows = slice(h * V_EXT, (h + 1) * V_EXT)
            acc_ref[rows, :] = alpha * acc_ref[rows, :] + _dot(v_ref[0, c, rows, :], p)

        _pipelined_heads(MLA_HEADS * len(chunks), scores, softmax, weighted_values, *MLA_PIPE)
        for h in range(MLA_HEADS):
            num = acc_ref[h * V_EXT:h * V_EXT + MLA_V, :]
            out_ref[h * MLA_V:(h + 1) * MLA_V, :] = num / acc_ref[h * V_EXT + MLA_V:h * V_EXT + MLA_V + 1, :]
        o_ref[0] = out_ref[...].T.astype(bf16)

    if off == 0:
        qt = pl.program_id(1)
        pl.when(qt == 0)(lambda: attend([0]))
        pl.when(qt != 0)(lambda: attend(list(range(nchunk))))
    else:
        attend(list(range(nchunk)))


def _mla(q_t, k, v_c, off):
    B, _, Lc = q_t.shape
    T = Lc // TM
    nq = T - off
    return pl.pallas_call(
        functools.partial(_mla_kernel, off=off),
        grid=(B, nq),
        in_specs=[
            pl.BlockSpec((1, MLA_HEADS * MLA_PAD, TM), lambda b, i: (b, 0, i + off)),
            pl.BlockSpec((1, Lc, MLA_HEADS * MLA_PAD), lambda b, i: (b, 0, 0)),
            pl.BlockSpec((1, T, MLA_HEADS * V_EXT, TM), lambda b, i: (b, 0, 0, 0)),
        ],
        out_specs=pl.BlockSpec((1, TM, BRANCH_W), lambda b, i: (b, i, 0)),
        out_shape=jax.ShapeDtypeStruct((B, nq * TM, BRANCH_W), bf16),
        scratch_shapes=[pltpu.VMEM((MLA_HEADS * V_EXT, TM), f32), pltpu.VMEM((BRANCH_W, TM), f32)],
        compiler_params=_cparams("parallel", "arbitrary"),
        name="mla_attn",
    )(q_t, k, v_c)


def _merge_kernel(c_ref, x_ref, a_ref, s_ref, ga_ref, ona_ref, osw_ref, oml_ref, wg_ref, wb_ref, wo_ref, o_ref,
                  *, off):
    x = _stream_tile(c_ref, x_ref, pl.program_id(1) + off)
    d = x.shape[1]
    hb = _norm_mod(x, a_ref[0], s_ref[0]).astype(bf16)
    z = None
    for n, o_n in enumerate((ona_ref, osw_ref, oml_ref)):
        g = jax.nn.sigmoid(_dot(hb, wg_ref[:, n * d:(n + 1) * d]))
        y = g * _dot(o_n[0], wb_ref[n])
        z = y if z is None else z + y
    o_ref[0] = x + ga_ref[0] * _dot(z.astype(bf16), wo_ref[...])


def _merge(c_src, x_src, a_row, s_row, g_row, o_na, o_sw, o_ml, w_gate, w_branch, w_out, off):
    B, _, D = x_src.shape
    nt = x_src.shape[1] // TM + (0 if c_src is x_src else 1) - off
    nb = a_row.shape[0] - 1
    row = lambda b, i: (jnp.where(i + off == 0, nb, b), 0, 0)
    ospec = pl.BlockSpec((1, TM, BRANCH_W), lambda b, i: (b, i, 0))
    return pl.pallas_call(
        functools.partial(_merge_kernel, off=off),
        grid=(B, nt),
        in_specs=[
            *_stream_specs(c_src, x_src, off),
            pl.BlockSpec((1, 1, D), row), pl.BlockSpec((1, 1, D), row), pl.BlockSpec((1, 1, D), row),
            ospec, ospec, ospec,
            _const_spec(w_gate.shape), _const_spec(w_branch.shape), _const_spec(w_out.shape),
        ],
        out_specs=pl.BlockSpec((1, TM, D), lambda b, i: (b, i, 0)),
        out_shape=jax.ShapeDtypeStruct((B, nt * TM, D), f32),
        compiler_params=_cparams("parallel", "parallel"),
        name="merge",
    )(c_src, x_src, a_row, s_row, g_row, o_na, o_sw, o_ml, w_gate, w_branch, w_out)


def _ffn_kernel(x_ref, xp_ref, xn_ref, a_ref, s_ref, gf_ref, wup_ref, cw_ref, cb_ref, wdn_ref, o_ref, u_ref,
                *, ctx_tiles):
    tm = x_ref.shape[1]
    halo = xp_ref.shape[1]
    nc = wup_ref.shape[0]
    t = pl.program_id(1)
    nt = pl.num_programs(1)
    x = x_ref[0]
    xe = jnp.concatenate([xp_ref[0], x, xn_ref[0]], axis=0)
    hb = _norm_mod(xe, a_ref[0], s_ref[0]).astype(bf16)
    left_ok = jnp.where(jnp.logical_and(t != 0, t != ctx_tiles), 1.0, 0.0)
    right_ok = jnp.where(jnp.logical_and(t != nt - 1, t != ctx_tiles - 1), 1.0, 0.0)

    def up(c):
        u = _dot(hb, wup_ref[c])
        slot = u_ref.at[c % 2]
        slot[:halo] = u[:halo] * left_ok
        slot[halo:halo + tm] = u[halo:halo + tm]
        slot[halo + tm:] = u[halo + tm:] * right_ok

    def conv_act(c):
        slot = u_ref.at[c % 2]
        w = cw_ref[c]
        uc = (w[0:1] * slot[halo - 1:halo - 1 + tm] + w[1:2] * slot[halo:halo + tm]
              + w[2:3] * slot[halo + 1:halo + 1 + tm] + cb_ref[c])
        g, v = uc[:, :FF_CHUNK], uc[:, FF_CHUNK:]
        return (g * jax.nn.sigmoid(g) * v).astype(bf16)

    acc = jnp.zeros(x.shape, f32)
    pending = []
    up(0)
    for c in range(nc):
        if c + 1 < nc:
            up(c + 1)
        pending.append((c, conv_act(c)))
        if len(pending) > FF_DOWN_BEHIND:
            j, act = pending.pop(0)
            acc = acc + _dot(act, wdn_ref[j])
    for j, act in pending:
        acc = acc + _dot(act, wdn_ref[j])
    o_ref[0] = x + gf_ref[0] * acc


def _ffn(x1, a_row, s_row, g_row, w_up_c, conv_w_c, conv_b_c, w_down_c, ctx_tiles):
    B, Lt, D = x1.shape
    nt = Lt // TM
    nb = a_row.shape[0] - 1
    per = TM // SUBLANES
    last = Lt // SUBLANES - 1
    row = lambda b, t: (jnp.where(t < ctx_tiles, nb, b), 0, 0)
    return pl.pallas_call(
        functools.partial(_ffn_kernel, ctx_tiles=ctx_tiles),
        grid=(B, nt),
        in_specs=[
            pl.BlockSpec((1, TM, D), lambda b, t: (b, t, 0)),
            pl.BlockSpec((1, SUBLANES, D), lambda b, t: (b, jnp.maximum(t * per - 1, 0), 0)),
            pl.BlockSpec((1, SUBLANES, D), lambda b, t: (b, jnp.minimum((t + 1) * per, last), 0)),
            pl.BlockSpec((1, 1, D), row), pl.BlockSpec((1, 1, D), row), pl.BlockSpec((1, 1, D), row),
            _const_spec(w_up_c.shape), _const_spec(conv_w_c.shape), _const_spec(conv_b_c.shape),
            _const_spec(w_down_c.shape),
        ],
        out_specs=pl.BlockSpec((1, TM, D), lambda b, t: (b, t, 0)),
        out_shape=jax.ShapeDtypeStruct((B, Lt, D), f32),
        scratch_shapes=[pltpu.VMEM((2, TM + 2 * SUBLANES, 2 * FF_CHUNK), f32)],
        compiler_params=_cparams("parallel", "arbitrary"),
        name="ffn",
    )(x1, x1, x1, a_row, s_row, g_row, w_up_c, conv_w_c, conv_b_c, w_down_c)


def _rope_table(seq, ctx_len, n):
    t = jnp.arange(seq)
    inv = ROPE_BASE ** (-jnp.arange(n, dtype=f32) / n)
    ar = (t // GRID_W).astype(f32)[None, :] * inv[:, None]
    ac = (t % GRID_W).astype(f32)[None, :] * inv[:, None]
    lat = jnp.concatenate([jnp.cos(ar), jnp.sin(ar), jnp.cos(ac), jnp.sin(ac)], axis=0)
    one, zero = jnp.ones((n, ctx_len), f32), jnp.zeros((n, ctx_len), f32)
    return jnp.concatenate([jnp.concatenate([one, zero, one, zero], axis=0), lat], axis=1)


def _chunked_ffn_weights(w_up, conv_w, conv_b, w_down):
    d_ff = w_down.shape[0]
    nc = d_ff // FF_CHUNK
    pair = lambda a: jnp.concatenate([a[..., :d_ff].reshape(a.shape[:-1] + (nc, FF_CHUNK)),
                                      a[..., d_ff:].reshape(a.shape[:-1] + (nc, FF_CHUNK))], axis=-1)
    w_up_c = jnp.moveaxis(pair(w_up), 1, 0).astype(bf16)
    conv_w_c = jnp.moveaxis(pair(conv_w), 1, 0)
    conv_b_c = pair(conv_b)[:, None, :]
    w_down_c = w_down.reshape(nc, FF_CHUNK, w_down.shape[1]).astype(bf16)
    return w_up_c, conv_w_c, conv_b_c, w_down_c


def kernel(x, c, ctx, c_ctx, w_ada, b_ada, g_mix, g_ffn, w_in, na_q_norm, na_k_norm, na_rpb, sw_q_norm, sw_k_norm,
           sw_sink, mla_q_rank_norm, mla_kv_rank_norm, w_uq, w_ukv, mla_q_norm, mla_k_norm, w_branch, w_out,
           w_up, conv_w, conv_b, w_down):
    B, S, D = x.shape
    C = ctx.shape[1]
    depth = w_ada.shape[0]
    assert C == TM and S % TM == 0 and GRID_W * NA_QROWS == TM and (S // GRID_W) * GRID_W == S
    assert S // GRID_W >= NA_KEYS // GRID_W and w_down.shape[1] % FF_CHUNK == 0
    rows = S // GRID_W

    c_src, x_src = ctx, x
    n_mod = -(-(B + 1) // SUBLANES) * SUBLANES
    c_all = jnp.zeros((n_mod, D), f32).at[:B].set(c).at[B].set(c_ctx)
    zero = jnp.zeros_like(g_mix)
    g_sel = jnp.stack([zero, g_mix, zero, zero, g_ffn, zero], axis=1)[:, :, None, :]
    mod = _ada(c_all, w_ada, b_ada, g_sel)

    rope_sw = _rope_table(S, C, HEAD_DIM // 4)
    rope_ml = _rope_table(S, C, MLA_ROPE // 4)
    group = SW_Q_HEADS // SW_KV_HEADS

    for l in range(depth):
        off = 0 if l < depth - 1 else 1
        m6 = mod[l, :B + 1].reshape(B + 1, 6, 1, D)
        shift_a, a_mix, gate_a, shift_f, a_ffn, gate_f = (m6[:, j] for j in range(6))
        w_in_t = w_in[l][:, :N_STREAM].T.astype(bf16)
        gains = jnp.concatenate([na_q_norm[l], na_k_norm[l], sw_q_norm[l], sw_k_norm[l], mla_q_rank_norm[l],
                                 mla_kv_rank_norm[l], mla_q_norm[l], mla_k_norm[l]])[:, None]
        (na_q, na_k, na_v, sw_q, sw_k, sw_v, m_q, m_k, m_v) = _proj(
            c_src, x_src, a_mix, shift_a, w_in_t, w_uq[l].T.astype(bf16), w_ukv[l].T.astype(bf16),
            gains, rope_sw, rope_ml)
        o_na = _na(na_q, na_k, na_v, _na_bias_table(na_rpb[l], rows), off)
        sink_rows = jnp.broadcast_to((sw_sink[l] * LOG2E)[:, None, None], (SW_Q_HEADS, 1, TM))
        o_sw = _sw(sw_q, sw_k, sw_v, sink_rows, off, C)
        o_ml = _mla(m_q, m_k, m_v, off)
        x1 = _merge(c_src, x_src, a_mix, shift_a, gate_a, o_na, o_sw, o_ml, w_in[l][:, N_STREAM:].astype(bf16),
                    w_branch[l].astype(bf16), w_out[l].astype(bf16), off)
        xc = _ffn(x1, a_ffn, shift_f, gate_f, *_chunked_ffn_weights(w_up[l], conv_w[l], conv_b[l], w_down[l]),
                  ctx_tiles=1 - off)
        c_src = x_src = xc
    return xc
```

```python
import functools

import jax
import jax.numpy as jnp
from jax import lax
from jax.experimental import pallas as pl
from jax.experimental.pallas import tpu as pltpu

f32 = jnp.float32
bf16 = jnp.bfloat16

GRID_W = 64
HEAD_DIM = 64
ROPE_BASE = 10000.0
NORM_EPS = 1e-6
NEG_INF = -1e30
NA_HEADS = 8
NA_WIN_ROWS = 8
NA_WIN_COLS = 16
SW_Q_HEADS = 8
SW_KV_HEADS = 2
SW_WINDOW = 128
MLA_HEADS = 8
MLA_Q_RANK = 384
MLA_KV_RANK = 256
MLA_NOPE = 64
MLA_ROPE = 32
MLA_V = 64
N_BRANCH = 3
BRANCH_W = 512
CONV_W = 3

LANES = 128
SUBLANES = 8
TM = 256
MLA_PAD = 128
LOCAL_PIPE = (4, 3)
MLA_PIPE = (4, 3)
FF_CHUNK = 256
FF_DOWN_BEHIND = 2
NA_QROWS = TM // GRID_W
NA_KEYS = 3 * TM
BF16_ROWS = 16
V_EXT = HEAD_DIM + BF16_ROWS
LOG2E = 1.4426950408889634
VMEM_LIMIT = 56 * 1024 * 1024

_SEG = {}
_o = 0
NA_W = NA_HEADS * HEAD_DIM
SW_Q_W = SW_Q_HEADS * HEAD_DIM
SW_KV_W = SW_KV_HEADS * HEAD_DIM
MLA_W = MLA_HEADS * MLA_PAD
for _name, _n in (("na_q", NA_W), ("na_k", NA_W), ("na_v", NA_W), ("sw_q", SW_Q_W), ("sw_k", SW_KV_W),
                  ("sw_v", SW_KV_W), ("c_q", MLA_Q_RANK), ("c_kv", MLA_KV_RANK), ("k_r", MLA_ROPE)):
    _SEG[_name] = (_o, _o + _n)
    _o += _n
N_STREAM = _o
_GN = {}
_o = 0
for _name, _n in (("na_q", 64), ("na_k", 64), ("sw_q", 64), ("sw_k", 64), ("mla_qr", MLA_Q_RANK),
                  ("mla_kvr", MLA_KV_RANK), ("mla_q", 96), ("mla_k", 96)):
    _GN[_name] = (_o, _o + _n)
    _o += _n
N_GAIN = _o


def _dot(a, b):
    return jnp.dot(a, b, preferred_element_type=f32)


def _cparams(*sem):
    return pltpu.CompilerParams(dimension_semantics=sem, vmem_limit_bytes=VMEM_LIMIT)


def _const_spec(shape):
    nd = len(shape)
    return pl.BlockSpec(shape, lambda *_: (0,) * nd, pipeline_mode=pl.Buffered(1))


def _ada_kernel(c_ref, w_ref, b_ref, g_ref, o_ref):
    n = pl.program_id(1)
    c = c_ref[...]
    sc = (c * jax.nn.sigmoid(c)).astype(bf16)
    mod = _dot(sc, w_ref[0].astype(bf16)) + b_ref[0]
    fold = jnp.logical_or(n == 1, n == 4)
    o_ref[0] = jnp.where(fold, g_ref[0, 0] * (1.0 + mod), mod)


def _ada(c_all, w_ada, b_ada, g_sel):
    L, D, _ = w_ada.shape
    R = c_all.shape[0]
    return pl.pallas_call(
        _ada_kernel,
        grid=(L, 6),
        in_specs=[
            pl.BlockSpec((R, D), lambda l, n: (0, 0)),
            pl.BlockSpec((1, D, D), lambda l, n: (l, 0, n)),
            pl.BlockSpec((1, 1, D), lambda l, n: (l, 0, n)),
            pl.BlockSpec((1, 1, 1, D), lambda l, n: (l, n, 0, 0)),
        ],
        out_specs=pl.BlockSpec((1, R, D), lambda l, n: (l, 0, n)),
        out_shape=jax.ShapeDtypeStruct((L, R, 6 * D), f32),
        compiler_params=_cparams("parallel", "parallel"),
        name="ada",
    )(c_all, w_ada, b_ada.reshape(L, 1, 6 * D), g_sel)


def _norm_mod(x, a, s):
    ms = jnp.mean(x * x, axis=-1, keepdims=True)
    return (x * lax.rsqrt(ms + NORM_EPS)) * a + s


def _head_norm(y3, gain):
    ms = jnp.mean(y3 * y3, axis=1, keepdims=True)
    return y3 * lax.rsqrt(ms + NORM_EPS) * gain[None]


def _rope(t3, tab, lo, n):
    cr, sr, cc, sc = (tab[i * n:(i + 1) * n][None] for i in range(4))
    a, b, c, e = (t3[:, lo + i * n:lo + (i + 1) * n] for i in range(4))
    return [a * cr - b * sr, a * sr + b * cr, c * cc - e * sc, c * sc + e * cc]


def _pipelined_heads(n, scores, softmax, weighted_values, ahead, behind):
    s, pending = {}, []
    for h in range(min(ahead, n)):
        s[h] = scores(h)
    for h in range(n):
        if h + ahead < n:
            s[h + ahead] = scores(h + ahead)
        pending.append((h, softmax(h, s.pop(h))))
        if len(pending) > behind:
            weighted_values(*pending.pop(0))
    for item in pending:
        weighted_values(*item)


def _blockwise_attention(heads, groups, scores, values, acc_ref, out_ref, o_ref, pipe, sink=None):
    tq = acc_ref.shape[1]
    acc_ref[...] = jnp.zeros_like(acc_ref)
    ms = [jnp.full((1, tq), NEG_INF, f32) for _ in range(heads)]

    def softmax(u, s):
        h = u % heads
        m_old = ms[h]
        ms[h] = functools.reduce(jnp.maximum, [jnp.max(x, axis=0, keepdims=True) for x in s] + [m_old])
        return jnp.exp2(m_old - ms[h]), [jnp.exp2((x - ms[h]).astype(bf16)) for x in s]

    def weighted_values(u, item):
        h, g = u % heads, u // heads
        alpha, p = item
        rows = slice(h * V_EXT, (h + 1) * V_EXT)
        upd = functools.reduce(jnp.add, [_dot(v, x) for v, x in zip(values(h, g), p)])
        acc_ref[rows, :] = alpha * acc_ref[rows, :] + upd

    _pipelined_heads(heads * groups, lambda u: scores(u % heads, u // heads), softmax, weighted_values, *pipe)
    for h in range(heads):
        den = acc_ref[h * V_EXT + HEAD_DIM:h * V_EXT + HEAD_DIM + 1, :]
        if sink is not None:
            den = den + jnp.exp2(sink(h) - ms[h])
        out_ref[h * HEAD_DIM:(h + 1) * HEAD_DIM, :] = acc_ref[h * V_EXT:h * V_EXT + HEAD_DIM, :] / den
    o_ref[0] = out_ref[...].T.astype(bf16)


def _stream_tile(c_ref, x_ref, t):
    return jnp.where(t == 0, c_ref[0], x_ref[0])


def _stream_specs(c_src, x_src, off):
    shift = 0 if c_src is x_src else 1
    d = x_src.shape[2]
    return [pl.BlockSpec((1, TM, d), lambda b, i: (b, 0, 0)),
            pl.BlockSpec((1, TM, d), lambda b, i: (b, jnp.maximum(i + off - shift, 0), 0))]


def _proj_kernel(c_ref, x_ref, a_ref, s_ref, win_ref, wuq_ref, wukv_ref, gn_ref, rsw_ref, rml_ref,
                 naq_ref, nak_ref, nav_ref, swq_ref, swk_ref, swv_ref, mq_ref, mk_ref, mv_ref):
    tm = x_ref.shape[1]
    h = _norm_mod(_stream_tile(c_ref, x_ref, pl.program_id(1)), a_ref[0], s_ref[0])
    ht = h.T.astype(bf16)

    def seg(name):
        lo, hi = _SEG[name]
        return _dot(win_ref[lo:hi, :], ht)

    def gain(name):
        lo, hi = _GN[name]
        return gn_ref[lo:hi, :]

    def with_ones(v, heads):
        v3 = v.reshape(heads, HEAD_DIM, tm)
        ones = jnp.ones((heads, BF16_ROWS, tm), f32)
        return jnp.concatenate([v3, ones], axis=1).reshape(heads * V_EXT, tm).astype(bf16)

    def rank_norm(y, name):
        return (y * lax.rsqrt(jnp.mean(y * y, axis=0, keepdims=True) + NORM_EPS) * gain(name)).astype(bf16)

    y_cq, y_ckv, kr = seg("c_q"), seg("c_kv"), seg("k_r")
    y_naq, y_nak = seg("na_q"), seg("na_k")
    mq = _dot(wuq_ref[...], rank_norm(y_cq, "mla_qr"))
    kv = _dot(wukv_ref[...], rank_norm(y_ckv, "mla_kvr"))
    y_nav, y_swq, y_swk, y_swv = seg("na_v"), seg("sw_q"), seg("sw_k"), seg("sw_v")

    qs = HEAD_DIM ** -0.5 * LOG2E
    q = _head_norm(y_naq.reshape(NA_HEADS, HEAD_DIM, tm), gain("na_q")) * qs
    naq_ref[0] = q.reshape(NA_HEADS * HEAD_DIM, tm).astype(bf16)
    k = _head_norm(y_nak.reshape(NA_HEADS, HEAD_DIM, tm), gain("na_k"))
    nak_ref[0] = k.reshape(NA_HEADS * HEAD_DIM, tm).T.astype(bf16)
    nav_ref[0, 0] = with_ones(y_nav, NA_HEADS)

    rml = rml_ref[...]
    nr = MLA_ROPE // 4
    dq = MLA_NOPE + MLA_ROPE
    pad = jnp.zeros((MLA_HEADS, MLA_PAD - dq, tm), f32)
    mq = _head_norm(mq.reshape(MLA_HEADS, dq, tm), gain("mla_q"))
    mq = jnp.concatenate([mq[:, :MLA_NOPE]] + _rope(mq, rml, MLA_NOPE, nr), axis=1) * (dq ** -0.5 * LOG2E)
    mq_ref[0] = jnp.concatenate([mq, pad], axis=1).reshape(MLA_HEADS * MLA_PAD, tm).astype(bf16)

    kv = kv.reshape(MLA_HEADS, MLA_NOPE + MLA_V, tm)
    kn, v = kv[:, :MLA_NOPE], kv[:, MLA_NOPE:]
    ss = jnp.sum(kn * kn, axis=1, keepdims=True) + jnp.sum(kr * kr, axis=0, keepdims=True)[None]
    r = lax.rsqrt(ss / dq + NORM_EPS)
    gk = gain("mla_k")
    kn = kn * r * gk[:MLA_NOPE][None]
    krh = kr[None] * r * gk[MLA_NOPE:][None]
    mk = jnp.concatenate([kn] + _rope(krh, rml, 0, nr) + [pad], axis=1)
    mk_ref[0] = mk.reshape(MLA_HEADS * MLA_PAD, tm).T.astype(bf16)
    mv_ref[0, 0] = with_ones(v.reshape(MLA_HEADS * MLA_V, tm), MLA_HEADS)

    rsw = rsw_ref[...]
    nq = HEAD_DIM // 4
    q = _head_norm(y_swq.reshape(SW_Q_HEADS, HEAD_DIM, tm), gain("sw_q"))
    q = jnp.concatenate(_rope(q, rsw, 0, nq), axis=1) * qs
    swq_ref[0] = q.reshape(SW_Q_HEADS * HEAD_DIM, tm).astype(bf16)
    k = _head_norm(y_swk.reshape(SW_KV_HEADS, HEAD_DIM, tm), gain("sw_k"))
    k = jnp.concatenate(_rope(k, rsw, 0, nq), axis=1)
    swk_ref[0] = k.reshape(SW_KV_HEADS * HEAD_DIM, tm).T.astype(bf16)
    v = with_ones(y_swv, SW_KV_HEADS)
    for j in range(tm // LANES):
        swv_ref[0, j] = v[:, j * LANES:(j + 1) * LANES]


def _proj(c_src, x_src, a_row, s_row, win_t, wuq_t, wukv_t, gains, rope_sw, rope_ml):
    B, _, D = x_src.shape
    T = x_src.shape[1] // TM + (0 if c_src is x_src else 1)
    Lc = T * TM
    nb = a_row.shape[0] - 1
    row = lambda b, t: (jnp.where(t == 0, nb, b), 0, 0)
    fm = lambda n: pl.BlockSpec((1, n, TM), lambda b, t: (b, 0, t))
    tk = lambda n: pl.BlockSpec((1, TM, n), lambda b, t: (b, t, 0))
    sds = jax.ShapeDtypeStruct
    return pl.pallas_call(
        _proj_kernel,
        grid=(B, T),
        in_specs=[
            *_stream_specs(c_src, x_src, 0),
            pl.BlockSpec((1, 1, D), row),
            pl.BlockSpec((1, 1, D), row),
            _const_spec(win_t.shape), _const_spec(wuq_t.shape), _const_spec(wukv_t.shape),
            _const_spec(gains.shape),
            pl.BlockSpec((rope_sw.shape[0], TM), lambda b, t: (0, t)),
            pl.BlockSpec((rope_ml.shape[0], TM), lambda b, t: (0, t)),
        ],
        out_specs=[
            fm(NA_W), tk(NA_W), pl.BlockSpec((1, 1, NA_HEADS * V_EXT, TM), lambda b, t: (b, t, 0, 0)),
            fm(SW_Q_W), tk(SW_KV_W),
            pl.BlockSpec((1, TM // LANES, SW_KV_HEADS * V_EXT, LANES), lambda b, t: (b, t, 0, 0)),
            fm(MLA_W), tk(MLA_W),
            pl.BlockSpec((1, 1, MLA_HEADS * V_EXT, TM), lambda b, t: (b, t, 0, 0)),
        ],
        out_shape=[
            sds((B, NA_W, Lc), bf16), sds((B, Lc, NA_W), bf16), sds((B, T, NA_HEADS * V_EXT, TM), bf16),
            sds((B, SW_Q_W, Lc), bf16), sds((B, Lc, SW_KV_W), bf16),
            sds((B, Lc // LANES, SW_KV_HEADS * V_EXT, LANES), bf16),
            sds((B, MLA_W, Lc), bf16), sds((B, Lc, MLA_W), bf16),
            sds((B, T, MLA_HEADS * V_EXT, TM), bf16),
        ],
        compiler_params=_cparams("parallel", "parallel"),
        name="proj",
    )(c_src, x_src, a_row, s_row, win_t, wuq_t, wukv_t, gains, rope_sw, rope_ml)


def _na_kernel(q_ref, kc_ref, k0_ref, k1_ref, k2_ref, vc_ref, v0_ref, v1_ref, v2_ref, bias_ref, o_ref,
               acc_ref, out_ref):
    tq = q_ref.shape[2]
    zeros = jnp.zeros((HEAD_DIM, tq), bf16)
    k_refs = (kc_ref, k0_ref, k1_ref, k2_ref)
    v_refs = (vc_ref, v0_ref, v1_ref, v2_ref)

    def scores(h, g):
        pair = slice((h // 2) * LANES, (h // 2 + 1) * LANES)
        qh = q_ref[0, h * HEAD_DIM:(h + 1) * HEAD_DIM, :]
        qpad = jnp.concatenate([qh, zeros] if h % 2 == 0 else [zeros, qh], axis=0)
        s = _dot(k_refs[g][0, :, pair], qpad)
        return [s if g == 0 else s + bias_ref[0, h, (g - 1) * TM:g * TM, :]]

    def values(h, g):
        return [v_refs[g][0, 0, h * V_EXT:(h + 1) * V_EXT, :]]

    _blockwise_attention(NA_HEADS, len(k_refs), scores, values, acc_ref, out_ref, o_ref, LOCAL_PIPE)


def _na(q_t, k, v_c, bias, off):
    B, _, Lc = q_t.shape
    T = Lc // TM
    nq = T - off
    nblk = T - 1

    def base(i):
        return 1 + jnp.clip(i + off - 2, 0, nblk - 3)

    def btype(i):
        qt = i + off
        return jnp.where(qt == 0, 3, jnp.where(qt == 1, 0, jnp.where(qt == T - 1, 2, 1)))

    kspec = lambda j: pl.BlockSpec((1, TM, NA_W), lambda b, i: (b, base(i) + j, 0))
    vrows = NA_HEADS * V_EXT
    vspec = lambda j: pl.BlockSpec((1, 1, vrows, TM), lambda b, i: (b, base(i) + j, 0, 0))
    return pl.pallas_call(
        _na_kernel,
        grid=(B, nq),
        in_specs=[
            pl.BlockSpec((1, NA_W, TM), lambda b, i: (b, 0, i + off)),
            pl.BlockSpec((1, TM, NA_W), lambda b, i: (b, 0, 0)), kspec(0), kspec(1), kspec(2),
            pl.BlockSpec((1, 1, vrows, TM), lambda b, i: (b, 0, 0, 0)), vspec(0), vspec(1), vspec(2),
            pl.BlockSpec((1, NA_HEADS, NA_KEYS, TM), lambda b, i: (btype(i), 0, 0, 0)),
        ],
        out_specs=pl.BlockSpec((1, TM, BRANCH_W), lambda b, i: (b, i, 0)),
        out_shape=jax.ShapeDtypeStruct((B, nq * TM, BRANCH_W), bf16),
        scratch_shapes=[pltpu.VMEM((BRANCH_W // HEAD_DIM * V_EXT, TM), f32), pltpu.VMEM((BRANCH_W, TM), f32)],
        compiler_params=_cparams("parallel", "arbitrary"),
        name="na_attn",
    )(q_t, k, k, k, k, v_c, v_c, v_c, v_c, bias)


def _na_bias_table(rpb, rows):
    col = jnp.arange(GRID_W)
    dc = jnp.clip(col[:, None] - col[None, :], -(NA_WIN_COLS - 1), NA_WIN_COLS - 1) + (NA_WIN_COLS - 1)
    onehot = (dc[None] == jnp.arange(2 * NA_WIN_COLS - 1)[:, None, None]).astype(f32)
    tile = jnp.einsum("hrd,dkq->hrkq", rpb.astype(f32) * LOG2E, onehot, precision=lax.Precision.HIGHEST)
    c0 = jnp.clip(col - NA_WIN_COLS // 2, 0, GRID_W - NA_WIN_COLS)
    in_win = (col[:, None] >= c0[None, :]) & (col[:, None] < c0[None, :] + NA_WIN_COLS)
    tile = jnp.where(in_win, tile, NEG_INF)
    masked = jnp.full((rpb.shape[0], GRID_W, GRID_W), NEG_INF, f32)
    krows = NA_KEYS // GRID_W

    def one(r0, ks):
        out = []
        for kr in range(ks, ks + krows):
            parts = []
            for qr in range(r0, r0 + NA_QROWS):
                start = min(max(qr - NA_WIN_ROWS // 2, 0), rows - NA_WIN_ROWS)
                parts.append(tile[:, kr - qr + NA_WIN_ROWS - 1] if start <= kr < start + NA_WIN_ROWS else masked)
            out.append(jnp.concatenate(parts, axis=2))
        return jnp.concatenate(out, axis=1)

    tabs = [one(0, 0), one(NA_QROWS, 0), one(rows - NA_QROWS, rows - krows)]
    tabs.append(jnp.full_like(tabs[0], NEG_INF))
    return jnp.stack(tabs)


def _sw_kernel(q_ref, kc_ref, k0_ref, k1_ref, k2_ref, k3_ref, vc_ref, v0_ref, v1_ref, v2_ref, v3_ref,
               sink_ref, o_ref, acc_ref, out_ref, *, off, ctx_len, total_len):
    tq = q_ref.shape[2]
    qt = pl.program_id(1) + off
    group = SW_Q_HEADS // SW_KV_HEADS
    k_refs = (k0_ref, k1_ref, k2_ref, k3_ref)
    v_refs = (v0_ref, v1_ref, v2_ref, v3_ref)
    kk = lax.broadcasted_iota(jnp.int32, (LANES, tq), 0)
    qpos = qt * tq + lax.broadcasted_iota(jnp.int32, (LANES, tq), 1)
    madd = []
    for i in range(4):
        kpos = (qt * (tq // LANES) - 1 + i) * LANES + kk
        ok = (kpos >= ctx_len) & (kpos < total_len) & (jnp.abs(kpos - qpos) <= SW_WINDOW) & (qt > 0)
        madd.append(jnp.where(ok, 0.0, NEG_INF).astype(f32))
    zeros = jnp.zeros((HEAD_DIM, tq), bf16)

    nctx = kc_ref.shape[1] // LANES
    pairs = len(k_refs) // 2

    def scores(h, g):
        qh = q_ref[0, h * HEAD_DIM:(h + 1) * HEAD_DIM, :]
        qpad = jnp.concatenate([qh, zeros] if h // group == 0 else [zeros, qh], axis=0)
        if g == 0:
            return [_dot(kc_ref[0, j * LANES:(j + 1) * LANES, :], qpad) for j in range(nctx)]
        return [_dot(k_refs[i][0], qpad) + madd[i] for i in (2 * g - 2, 2 * g - 1)]

    def values(h, g):
        kv_rows = slice((h // group) * V_EXT, (h // group + 1) * V_EXT)
        if g == 0:
            return [vc_ref[0, j, kv_rows, :] for j in range(nctx)]
        return [v_refs[i][0, 0, kv_rows, :] for i in (2 * g - 2, 2 * g - 1)]

    _blockwise_attention(SW_Q_HEADS, 1 + pairs, scores, values, acc_ref, out_ref, o_ref, LOCAL_PIPE,
                         sink=lambda h: sink_ref[h])


def _sw(q_t, k, v_c, sink_rows, off, ctx_len):
    B, _, Lc = q_t.shape
    T = Lc // TM
    nq = T - off
    per = TM // LANES
    lo, hi = ctx_len // LANES, Lc // LANES - 1

    def blk(i, j):
        return jnp.clip((i + off) * per - 1 + j, lo, hi)

    kspec = lambda j: pl.BlockSpec((1, LANES, SW_KV_W), lambda b, i: (b, blk(i, j), 0))
    vrows = SW_KV_HEADS * V_EXT
    vspec = lambda j: pl.BlockSpec((1, 1, vrows, LANES), lambda b, i: (b, blk(i, j), 0, 0))
    kern = functools.partial(_sw_kernel, off=off, ctx_len=ctx_len, total_len=Lc)
    return pl.pallas_call(
        kern,
        grid=(B, nq),
        in_specs=[
            pl.BlockSpec((1, SW_Q_W, TM), lambda b, i: (b, 0, i + off)),
            pl.BlockSpec((1, ctx_len, SW_KV_W), lambda b, i: (b, 0, 0)),
            kspec(0), kspec(1), kspec(2), kspec(3),
            pl.BlockSpec((1, ctx_len // LANES, vrows, LANES), lambda b, i: (b, 0, 0, 0)),
            vspec(0), vspec(1), vspec(2), vspec(3),
            pl.BlockSpec(sink_rows.shape, lambda b, i: (0, 0, 0)),
        ],
        out_specs=pl.BlockSpec((1, TM, BRANCH_W), lambda b, i: (b, i, 0)),
        out_shape=jax.ShapeDtypeStruct((B, nq * TM, BRANCH_W), bf16),
        scratch_shapes=[pltpu.VMEM((BRANCH_W // HEAD_DIM * V_EXT, TM), f32), pltpu.VMEM((BRANCH_W, TM), f32)],
        compiler_params=_cparams("parallel", "arbitrary"),
        name="sw_attn",
    )(q_t, k, k, k, k, k, v_c, v_c, v_c, v_c, v_c, sink_rows)


def _mla_kernel(q_ref, k_ref, v_ref, o_ref, acc_ref, out_ref, *, off):
    nchunk = v_ref.shape[1]
    tq = q_ref.shape[2]

    def attend(chunks):
        acc_ref[...] = jnp.zeros_like(acc_ref)
        ms = [jnp.full((1, tq), NEG_INF, f32) for _ in range(MLA_HEADS)]

        def scores(u):
            h, c = u % MLA_HEADS, chunks[u // MLA_HEADS]
            pair = slice((h // 2) * 2 * MLA_PAD, (h // 2 + 1) * 2 * MLA_PAD)
            qh = q_ref[0, h * MLA_PAD:(h + 1) * MLA_PAD, :]
            zeros = jnp.zeros_like(qh)
            qpad = jnp.concatenate([qh, zeros] if h % 2 == 0 else [zeros, qh], axis=0)
            return _dot(k_ref[0, c * TM:(c + 1) * TM, pair], qpad)

        def softmax(u, s):
            h = u % MLA_HEADS
            m_old = ms[h]
            ms[h] = jnp.maximum(m_old, jnp.max(s, axis=0, keepdims=True))
            return jnp.exp2(m_old - ms[h]), jnp.exp2(s - ms[h]).astype(bf16)

        def weighted_values(u, item):
            h, c = u % MLA_HEADS, chunks[u // MLA_HEADS]
            alpha, p = item
            rows = slice(h * V_EXT, (h + 1) * V_EXT)
            acc_ref[rows, :] = alpha * acc_ref[rows, :] + _dot(v_ref[0, c, rows, :], p)

        _pipelined_heads(MLA_HEADS * len(chunks), scores, softmax, weighted_values, *MLA_PIPE)
        for h in range(MLA_HEADS):
            num = acc_ref[h * V_EXT:h * V_EXT + MLA_V, :]
            out_ref[h * MLA_V:(h + 1) * MLA_V, :] = num / acc_ref[h * V_EXT + MLA_V:h * V_EXT + MLA_V + 1, :]
        o_ref[0] = out_ref[...].T.astype(bf16)

    if off == 0:
        qt = pl.program_id(1)
        pl.when(qt == 0)(lambda: attend([0]))
        pl.when(qt != 0)(lambda: attend(list(range(nchunk))))
    else:
        attend(list(range(nchunk)))


def _mla(q_t, k, v_c, off):
    B, _, Lc = q_t.shape
    T = Lc // TM
    nq = T - off
    return pl.pallas_call(
        functools.partial(_mla_kernel, off=off),
        grid=(B, nq),
        in_specs=[
            pl.BlockSpec((1, MLA_HEADS * MLA_PAD, TM), lambda b, i: (b, 0, i + off)),
            pl.BlockSpec((1, Lc, MLA_HEADS * MLA_PAD), lambda b, i: (b, 0, 0)),
            pl.BlockSpec((1, T, MLA_HEADS * V_EXT, TM), lambda b, i: (b, 0, 0, 0)),
        ],
        out_specs=pl.BlockSpec((1, TM, BRANCH_W), lambda b, i: (b, i, 0)),
        out_shape=jax.ShapeDtypeStruct((B, nq * TM, BRANCH_W), bf16),
        scratch_shapes=[pltpu.VMEM((MLA_HEADS * V_EXT, TM), f32), pltpu.VMEM((BRANCH_W, TM), f32)],
        compiler_params=_cparams("parallel", "arbitrary"),
        name="mla_attn",
    )(q_t, k, v_c)


def _merge_kernel(c_ref, x_ref, a_ref, s_ref, ga_ref, ona_ref, osw_ref, oml_ref, wg_ref, wb_ref, wo_ref, o_ref,
                  *, off):
    x = _stream_tile(c_ref, x_ref, pl.program_id(1) + off)
    d = x.shape[1]
    hb = _norm_mod(x, a_ref[0], s_ref[0]).astype(bf16)
    z = None
    for n, o_n in enumerate((ona_ref, osw_ref, oml_ref)):
        g = jax.nn.sigmoid(_dot(hb, wg_ref[:, n * d:(n + 1) * d]))
        y = g * _dot(o_n[0], wb_ref[n])
        z = y if z is None else z + y
    o_ref[0] = x + ga_ref[0] * _dot(z.astype(bf16), wo_ref[...])


def _merge(c_src, x_src, a_row, s_row, g_row, o_na, o_sw, o_ml, w_gate, w_branch, w_out, off):
    B, _, D = x_src.shape
    nt = x_src.shape[1] // TM + (0 if c_src is x_src else 1) - off
    nb = a_row.shape[0] - 1
    row = lambda b, i: (jnp.where(i + off == 0, nb, b), 0, 0)
    ospec = pl.BlockSpec((1, TM, BRANCH_W), lambda b, i: (b, i, 0))
    return pl.pallas_call(
        functools.partial(_merge_kernel, off=off),
        grid=(B, nt),
        in_specs=[
            *_stream_specs(c_src, x_src, off),
            pl.BlockSpec((1, 1, D), row), pl.BlockSpec((1, 1, D), row), pl.BlockSpec((1, 1, D), row),
            ospec, ospec, ospec,
            _const_spec(w_gate.shape), _const_spec(w_branch.shape), _const_spec(w_out.shape),
        ],
        out_specs=pl.BlockSpec((1, TM, D), lambda b, i: (b, i, 0)),
        out_shape=jax.ShapeDtypeStruct((B, nt * TM, D), f32),
        compiler_params=_cparams("parallel", "parallel"),
        name="merge",
    )(c_src, x_src, a_row, s_row, g_row, o_na, o_sw, o_ml, w_gate, w_branch, w_out)


def _ffn_kernel(x_ref, xp_ref, xn_ref, a_ref, s_ref, gf_ref, wup_ref, cw_ref, cb_ref, wdn_ref, o_ref, u_ref,
                *, ctx_tiles):
    tm = x_ref.shape[1]
    halo = xp_ref.shape[1]
    nc = wup_ref.shape[0]
    t = pl.program_id(1)
    nt = pl.num_programs(1)
    x = x_ref[0]
    xe = jnp.concatenate([xp_ref[0], x, xn_ref[0]], axis=0)
    hb = _norm_mod(xe, a_ref[0], s_ref[0]).astype(bf16)
    left_ok = jnp.where(jnp.logical_and(t != 0, t != ctx_tiles), 1.0, 0.0)
    right_ok = jnp.where(jnp.logical_and(t != nt - 1, t != ctx_tiles - 1), 1.0, 0.0)

    def up(c):
        u = _dot(hb, wup_ref[c])
        slot = u_ref.at[c % 2]
        slot[:halo] = u[:halo] * left_ok
        slot[halo:halo + tm] = u[halo:halo + tm]
        slot[halo + tm:] = u[halo + tm:] * right_ok

    def conv_act(c):
        slot = u_ref.at[c % 2]
        w = cw_ref[c]
        uc = (w[0:1] * slot[halo - 1:halo - 1 + tm] + w[1:2] * slot[halo:halo + tm]
              + w[2:3] * slot[halo + 1:halo + 1 + tm] + cb_ref[c])
        g, v = uc[:, :FF_CHUNK], uc[:, FF_CHUNK:]
        return (g * jax.nn.sigmoid(g) * v).astype(bf16)

    acc = jnp.zeros(x.shape, f32)
    pending = []
    up(0)
    for c in range(nc):
        if c + 1 < nc:
            up(c + 1)
        pending.append((c, conv_act(c)))
        if len(pending) > FF_DOWN_BEHIND:
            j, act = pending.pop(0)
            acc = acc + _dot(act, wdn_ref[j])
    for j, act in pending:
        acc = acc + _dot(act, wdn_ref[j])
    o_ref[0] = x + gf_ref[0] * acc


def _ffn(x1, a_row, s_row, g_row, w_up_c, conv_w_c, conv_b_c, w_down_c, ctx_tiles):
    B, Lt, D = x1.shape
    nt = Lt // TM
    nb = a_row.shape[0] - 1
    per = TM // SUBLANES
    last = Lt // SUBLANES - 1
    row = lambda b, t: (jnp.where(t < ctx_tiles, nb, b), 0, 0)
    return pl.pallas_call(
        functools.partial(_ffn_kernel, ctx_tiles=ctx_tiles),
        grid=(B, nt),
        in_specs=[
            pl.BlockSpec((1, TM, D), lambda b, t: (b, t, 0)),
            pl.BlockSpec((1, SUBLANES, D), lambda b, t: (b, jnp.maximum(t * per - 1, 0), 0)),
            pl.BlockSpec((1, SUBLANES, D), lambda b, t: (b, jnp.minimum((t + 1) * per, last), 0)),
            pl.BlockSpec((1, 1, D), row), pl.BlockSpec((1, 1, D), row), pl.BlockSpec((1, 1, D), row),
            _const_spec(w_up_c.shape), _const_spec(conv_w_c.shape), _const_spec(conv_b_c.shape),
            _const_spec(w_down_c.shape),
        ],
        out_specs=pl.BlockSpec((1, TM, D), lambda b, t: (b, t, 0)),
        out_shape=jax.ShapeDtypeStruct((B, Lt, D), f32),
        scratch_shapes=[pltpu.VMEM((2, TM + 2 * SUBLANES, 2 * FF_CHUNK), f32)],
        compiler_params=_cparams("parallel", "arbitrary"),
        name="ffn",
    )(x1, x1, x1, a_row, s_row, g_row, w_up_c, conv_w_c, conv_b_c, w_down_c)


def _rope_table(seq, ctx_len, n):
    t = jnp.arange(seq)
    inv = ROPE_BASE ** (-jnp.arange(n, dtype=f32) / n)
    ar = (t // GRID_W).astype(f32)[None, :] * inv[:, None]
    ac = (t % GRID_W).astype(f32)[None, :] * inv[:, None]
    lat = jnp.concatenate([jnp.cos(ar), jnp.sin(ar), jnp.cos(ac), jnp.sin(ac)], axis=0)
    one, zero = jnp.ones((n, ctx_len), f32), jnp.zeros((n, ctx_len), f32)
    return jnp.concatenate([jnp.concatenate([one, zero, one, zero], axis=0), lat], axis=1)


def _chunked_ffn_weights(w_up, conv_w, conv_b, w_down):
    d_ff = w_down.shape[0]
    nc = d_ff // FF_CHUNK
    pair = lambda a: jnp.concatenate([a[..., :d_ff].reshape(a.shape[:-1] + (nc, FF_CHUNK)),
                                      a[..., d_ff:].reshape(a.shape[:-1] + (nc, FF_CHUNK))], axis=-1)
    w_up_c = jnp.moveaxis(pair(w_up), 1, 0).astype(bf16)
    conv_w_c = jnp.moveaxis(pair(conv_w), 1, 0)
    conv_b_c = pair(conv_b)[:, None, :]
    w_down_c = w_down.reshape(nc, FF_CHUNK, w_down.shape[1]).astype(bf16)
    return w_up_c, conv_w_c, conv_b_c, w_down_c


def kernel(x, c, ctx, c_ctx, w_ada, b_ada, g_mix, g_ffn, w_in, na_q_norm, na_k_norm, na_rpb, sw_q_norm, sw_k_norm,
           sw_sink, mla_q_rank_norm, mla_kv_rank_norm, w_uq, w_ukv, mla_q_norm, mla_k_norm, w_branch, w_out,
           w_up, conv_w, conv_b, w_down):
    B, S, D = x.shape
    C = ctx.shape[1]
    depth = w_ada.shape[0]
    assert C == TM and S % TM == 0 and GRID_W * NA_QROWS == TM and (S // GRID_W) * GRID_W == S
    assert S // GRID_W >= NA_KEYS // GRID_W and w_down.shape[1] % FF_CHUNK == 0
    rows = S // GRID_W

    c_src, x_src = ctx, x
    n_mod = -(-(B + 1) // SUBLANES) * SUBLANES
    c_all = jnp.zeros((n_mod, D), f32).at[:B].set(c).at[B].set(c_ctx)
    zero = jnp.zeros_like(g_mix)
    g_sel = jnp.stack([zero, g_mix, zero, zero, g_ffn, zero], axis=1)[:, :, None, :]
    mod = _ada(c_all, w_ada, b_ada, g_sel)

    rope_sw = _rope_table(S, C, HEAD_DIM // 4)
    rope_ml = _rope_table(S, C, MLA_ROPE // 4)
    group = SW_Q_HEADS // SW_KV_HEADS

    for l in range(depth):
        off = 0 if l < depth - 1 else 1
        m6 = mod[l, :B + 1].reshape(B + 1, 6, 1, D)
        shift_a, a_mix, gate_a, shift_f, a_ffn, gate_f = (m6[:, j] for j in range(6))
        w_in_t = w_in[l][:, :N_STREAM].T.astype(bf16)
        gains = jnp.concatenate([na_q_norm[l], na_k_norm[l], sw_q_norm[l], sw_k_norm[l], mla_q_rank_norm[l],
                                 mla_kv_rank_norm[l], mla_q_norm[l], mla_k_norm[l]])[:, None]
        (na_q, na_k, na_v, sw_q, sw_k, sw_v, m_q, m_k, m_v) = _proj(
            c_src, x_src, a_mix, shift_a, w_in_t, w_uq[l].T.astype(bf16), w_ukv[l].T.astype(bf16),
            gains, rope_sw, rope_ml)
        o_na = _na(na_q, na_k, na_v, _na_bias_table(na_rpb[l], rows), off)
        sink_rows = jnp.broadcast_to((sw_sink[l] * LOG2E)[:, None, None], (SW_Q_HEADS, 1, TM))
        o_sw = _sw(sw_q, sw_k, sw_v, sink_rows, off, C)
        o_ml = _mla(m_q, m_k, m_v, off)
        x1 = _merge(c_src, x_src, a_mix, shift_a, gate_a, o_na, o_sw, o_ml, w_in[l][:, N_STREAM:].astype(bf16),
                    w_branch[l].astype(bf16), w_out[l].astype(bf16), off)
        xc = _ffn(x1, a_ffn, shift_f, gate_f, *_chunked_ffn_weights(w_up[l], conv_w[l], conv_b[l], w_down[l]),
                  ctx_tiles=1 - off)
        c_src = x_src = xc
    return xc
```

```python
import functools

import jax
import jax.numpy as jnp
from jax import lax
from jax.experimental import pallas as pl
from jax.experimental.pallas import tpu as pltpu

f32 = jnp.float32
bf16 = jnp.bfloat16

GRID_W = 64
HEAD_DIM = 64
ROPE_BASE = 10000.0
NORM_EPS = 1e-6
NEG_INF = -1e30
NA_HEADS = 8
NA_WIN_ROWS = 8
NA_WIN_COLS = 16
SW_Q_HEADS = 8
SW_KV_HEADS = 2
SW_WINDOW = 128
MLA_HEADS = 8
MLA_Q_RANK = 384
MLA_KV_RANK = 256
MLA_NOPE = 64
MLA_ROPE = 32
MLA_V = 64
N_BRANCH = 3
BRANCH_W = 512
CONV_W = 3

LANES = 128
SUBLANES = 8
TM = 256
MLA_PAD = 128
LOCAL_PIPE = (4, 3)
MLA_PIPE = (4, 3)
FF_CHUNK = 256
FF_DOWN_BEHIND = 2
NA_QROWS = TM // GRID_W
NA_KEYS = 3 * TM
BF16_ROWS = 16
V_EXT = HEAD_DIM + BF16_ROWS
LOG2E = 1.4426950408889634
VMEM_LIMIT = 56 * 1024 * 1024

_SEG = {}
_o = 0
NA_W = NA_HEADS * HEAD_DIM
SW_Q_W = SW_Q_HEADS * HEAD_DIM
SW_KV_W = SW_KV_HEADS * HEAD_DIM
MLA_W = MLA_HEADS * MLA_PAD
for _name, _n in (("na_q", NA_W), ("na_k", NA_W), ("na_v", NA_W), ("sw_q", SW_Q_W), ("sw_k", SW_KV_W),
                  ("sw_v", SW_KV_W), ("c_q", MLA_Q_RANK), ("c_kv", MLA_KV_RANK), ("k_r", MLA_ROPE)):
    _SEG[_name] = (_o, _o + _n)
    _o += _n
N_STREAM = _o
_GN = {}
_o = 0
for _name, _n in (("na_q", 64), ("na_k", 64), ("sw_q", 64), ("sw_k", 64), ("mla_qr", MLA_Q_RANK),
                  ("mla_kvr", MLA_KV_RANK), ("mla_q", 96), ("mla_k", 96)):
    _GN[_name] = (_o, _o + _n)
    _o += _n
N_GAIN = _o


def _dot(a, b):
    return jnp.dot(a, b, preferred_element_type=f32)


def _cparams(*sem):
    return pltpu.CompilerParams(dimension_semantics=sem, vmem_limit_bytes=VMEM_LIMIT)


def _const_spec(shape):
    nd = len(shape)
    return pl.BlockSpec(shape, lambda *_: (0,) * nd, pipeline_mode=pl.Buffered(1))


def _ada_kernel(c_ref, w_ref, b_ref, g_ref, o_ref):
    n = pl.program_id(1)
    c = c_ref[...]
    sc = (c * jax.nn.sigmoid(c)).astype(bf16)
    mod = _dot(sc, w_ref[0].astype(bf16)) + b_ref[0]
    fold = jnp.logical_or(n == 1, n == 4)
    o_ref[0] = jnp.where(fold, g_ref[0, 0] * (1.0 + mod), mod)


def _ada(c_all, w_ada, b_ada, g_sel):
    L, D, _ = w_ada.shape
    R = c_all.shape[0]
    return pl.pallas_call(
        _ada_kernel,
        grid=(L, 6),
        in_specs=[
            pl.BlockSpec((R, D), lambda l, n: (0, 0)),
            pl.BlockSpec((1, D, D), lambda l, n: (l, 0, n)),
            pl.BlockSpec((1, 1, D), lambda l, n: (l, 0, n)),
            pl.BlockSpec((1, 1, 1, D), lambda l, n: (l, n, 0, 0)),
        ],
        out_specs=pl.BlockSpec((1, R, D), lambda l, n: (l, 0, n)),
        out_shape=jax.ShapeDtypeStruct((L, R, 6 * D), f32),
        compiler_params=_cparams("parallel", "parallel"),
        name="ada",
    )(c_all, w_ada, b_ada.reshape(L, 1, 6 * D), g_sel)


def _norm_mod(x, a, s):
    ms = jnp.mean(x * x, axis=-1, keepdims=True)
    return (x * lax.rsqrt(ms + NORM_EPS)) * a + s


def _head_norm(y3, gain):
    ms = jnp.mean(y3 * y3, axis=1, keepdims=True)
    return y3 * lax.rsqrt(ms + NORM_EPS) * gain[None]


def _rope(t3, tab, lo, n):
    cr, sr, cc, sc = (tab[i * n:(i + 1) * n][None] for i in range(4))
    a, b, c, e = (t3[:, lo + i * n:lo + (i + 1) * n] for i in range(4))
    return [a * cr - b * sr, a * sr + b * cr, c * cc - e * sc, c * sc + e * cc]


def _pipelined_heads(n, scores, softmax, weighted_values, ahead, behind):
    s, pending = {}, []
    for h in range(min(ahead, n)):
        s[h] = scores(h)
    for h in range(n):
        if h + ahead < n:
            s[h + ahead] = scores(h + ahead)
        pending.append((h, softmax(h, s.pop(h))))
        if len(pending) > behind:
            weighted_values(*pending.pop(0))
    for item in pending:
        weighted_values(*item)


def _blockwise_attention(heads, groups, scores, values, acc_ref, out_ref, o_ref, pipe, sink=None):
    tq = acc_ref.shape[1]
    acc_ref[...] = jnp.zeros_like(acc_ref)
    ms = [jnp.full((1, tq), NEG_INF, f32) for _ in range(heads)]

    def softmax(u, s):
        h = u % heads
        m_old = ms[h]
        ms[h] = functools.reduce(jnp.maximum, [jnp.max(x, axis=0, keepdims=True) for x in s] + [m_old])
        return jnp.exp2(m_old - ms[h]), [jnp.exp2((x - ms[h]).astype(bf16)) for x in s]

    def weighted_values(u, item):
        h, g = u % heads, u // heads
        alpha, p = item
        rows = slice(h * V_EXT, (h + 1) * V_EXT)
        upd = functools.reduce(jnp.add, [_dot(v, x) for v, x in zip(values(h, g), p)])
        acc_ref[rows, :] = alpha * acc_ref[rows, :] + upd

    _pipelined_heads(heads * groups, lambda u: scores(u % heads, u // heads), softmax, weighted_values, *pipe)
    for h in range(heads):
        den = acc_ref[h * V_EXT + HEAD_DIM:h * V_EXT + HEAD_DIM + 1, :]
        if sink is not None:
            den = den + jnp.exp2(sink(h) - ms[h])
        out_ref[h * HEAD_DIM:(h + 1) * HEAD_DIM, :] = acc_ref[h * V_EXT:h * V_EXT + HEAD_DIM, :] / den
    o_ref[0] = out_ref[...].T.astype(bf16)


def _stream_tile(c_ref, x_ref, t):
    return jnp.where(t == 0, c_ref[0], x_ref[0])


def _stream_specs(c_src, x_src, off):
    shift = 0 if c_src is x_src else 1
    d = x_src.shape[2]
    return [pl.BlockSpec((1, TM, d), lambda b, i: (b, 0, 0)),
            pl.BlockSpec((1, TM, d), lambda b, i: (b, jnp.maximum(i + off - shift, 0), 0))]


def _proj_kernel(c_ref, x_ref, a_ref, s_ref, win_ref, wuq_ref, wukv_ref, gn_ref, rsw_ref, rml_ref,
                 naq_ref, nak_ref, nav_ref, swq_ref, swk_ref, swv_ref, mq_ref, mk_ref, mv_ref):
    tm = x_ref.shape[1]
    h = _norm_mod(_stream_tile(c_ref, x_ref, pl.program_id(1)), a_ref[0], s_ref[0])
    ht = h.T.astype(bf16)

    def seg(name):
        lo, hi = _SEG[name]
        return _dot(win_ref[lo:hi, :], ht)

    def gain(name):
        lo, hi = _GN[name]
        return gn_ref[lo:hi, :]

    def with_ones(v, heads):
        v3 = v.reshape(heads, HEAD_DIM, tm)
        ones = jnp.ones((heads, BF16_ROWS, tm), f32)
        return jnp.concatenate([v3, ones], axis=1).reshape(heads * V_EXT, tm).astype(bf16)

    def rank_norm(y, name):
        return (y * lax.rsqrt(jnp.mean(y * y, axis=0, keepdims=True) + NORM_EPS) * gain(name)).astype(bf16)

    y_cq, y_ckv, kr = seg("c_q"), seg("c_kv"), seg("k_r")
    y_naq, y_nak = seg("na_q"), seg("na_k")
    mq = _dot(wuq_ref[...], rank_norm(y_cq, "mla_qr"))
    kv = _dot(wukv_ref[...], rank_norm(y_ckv, "mla_kvr"))
    y_nav, y_swq, y_swk, y_swv = seg("na_v"), seg("sw_q"), seg("sw_k"), seg("sw_v")

    qs = HEAD_DIM ** -0.5 * LOG2E
    q = _head_norm(y_naq.reshape(NA_HEADS, HEAD_DIM, tm), gain("na_q")) * qs
    naq_ref[0] = q.reshape(NA_HEADS * HEAD_DIM, tm).astype(bf16)
    k = _head_norm(y_nak.reshape(NA_HEADS, HEAD_DIM, tm), gain("na_k"))
    nak_ref[0] = k.reshape(NA_HEADS * HEAD_DIM, tm).T.astype(bf16)
    nav_ref[0, 0] = with_ones(y_nav, NA_HEADS)

    rml = rml_ref[...]
    nr = MLA_ROPE // 4
    dq = MLA_NOPE + MLA_ROPE
    pad = jnp.zeros((MLA_HEADS, MLA_PAD - dq, tm), f32)
    mq = _head_norm(mq.reshape(MLA_HEADS, dq, tm), gain("mla_q"))
    mq = jnp.concatenate([mq[:, :MLA_NOPE]] + _rope(mq, rml, MLA_NOPE, nr), axis=1) * (dq ** -0.5 * LOG2E)
    mq_ref[0] = jnp.concatenate([mq, pad], axis=1).reshape(MLA_HEADS * MLA_PAD, tm).astype(bf16)

    kv = kv.reshape(MLA_HEADS, MLA_NOPE + MLA_V, tm)
    kn, v = kv[:, :MLA_NOPE], kv[:, MLA_NOPE:]
    ss = jnp.sum(kn * kn, axis=1, keepdims=True) + jnp.sum(kr * kr, axis=0, keepdims=True)[None]
    r = lax.rsqrt(ss / dq + NORM_EPS)
    gk = gain("mla_k")
    kn = kn * r * gk[:MLA_NOPE][None]
    krh = kr[None] * r * gk[MLA_NOPE:][None]
    mk = jnp.concatenate([kn] + _rope(krh, rml, 0, nr) + [pad], axis=1)
    mk_ref[0] = mk.reshape(MLA_HEADS * MLA_PAD, tm).T.astype(bf16)
    mv_ref[0, 0] = with_ones(v.reshape(MLA_HEADS * MLA_V, tm), MLA_HEADS)

    rsw = rsw_ref[...]
    nq = HEAD_DIM // 4
    q = _head_norm(y_swq.reshape(SW_Q_HEADS, HEAD_DIM, tm), gain("sw_q"))
    q = jnp.concatenate(_rope(q, rsw, 0, nq), axis=1) * qs
    swq_ref[0] = q.reshape(SW_Q_HEADS * HEAD_DIM, tm).astype(bf16)
    k = _head_norm(y_swk.reshape(SW_KV_HEADS, HEAD_DIM, tm), gain("sw_k"))
    k = jnp.concatenate(_rope(k, rsw, 0, nq), axis=1)
    swk_ref[0] = k.reshape(SW_KV_HEADS * HEAD_DIM, tm).T.astype(bf16)
    v = with_ones(y_swv, SW_KV_HEADS)
    for j in range(tm // LANES):
        swv_ref[0, j] = v[:, j * LANES:(j + 1) * LANES]


def _proj(c_src, x_src, a_row, s_row, win_t, wuq_t, wukv_t, gains, rope_sw, rope_ml):
    B, _, D = x_src.shape
    T = x_src.shape[1] // TM + (0 if c_src is x_src else 1)
    Lc = T * TM
    nb = a_row.shape[0] - 1
    row = lambda b, t: (jnp.where(t == 0, nb, b), 0, 0)
    fm = lambda n: pl.BlockSpec((1, n, TM), lambda b, t: (b, 0, t))
    tk = lambda n: pl.BlockSpec((1, TM, n), lambda b, t: (b, t, 0))
    sds = jax.ShapeDtypeStruct
    return pl.pallas_call(
        _proj_kernel,
        grid=(B, T),
        in_specs=[
            *_stream_specs(c_src, x_src, 0),
            pl.BlockSpec((1, 1, D), row),
            pl.BlockSpec((1, 1, D), row),
            _const_spec(win_t.shape), _const_spec(wuq_t.shape), _const_spec(wukv_t.shape),
            _const_spec(gains.shape),
            pl.BlockSpec((rope_sw.shape[0], TM), lambda b, t: (0, t)),
            pl.BlockSpec((rope_ml.shape[0], TM), lambda b, t: (0, t)),
        ],
        out_specs=[
            fm(NA_W), tk(NA_W), pl.BlockSpec((1, 1, NA_HEADS * V_EXT, TM), lambda b, t: (b, t, 0, 0)),
            fm(SW_Q_W), tk(SW_KV_W),
            pl.BlockSpec((1, TM // LANES, SW_KV_HEADS * V_EXT, LANES), lambda b, t: (b, t, 0, 0)),
            fm(MLA_W), tk(MLA_W),
            pl.BlockSpec((1, 1, MLA_HEADS * V_EXT, TM), lambda b, t: (b, t, 0, 0)),
        ],
        out_shape=[
            sds((B, NA_W, Lc), bf16), sds((B, Lc, NA_W), bf16), sds((B, T, NA_HEADS * V_EXT, TM), bf16),
            sds((B, SW_Q_W, Lc), bf16), sds((B, Lc, SW_KV_W), bf16),
            sds((B, Lc // LANES, SW_KV_HEADS * V_EXT, LANES), bf16),
            sds((B, MLA_W, Lc), bf16), sds((B, Lc, MLA_W), bf16),
            sds((B, T, MLA_HEADS * V_EXT, TM), bf16),
        ],
        compiler_params=_cparams("parallel", "parallel"),
        name="proj",
    )(c_src, x_src, a_row, s_row, win_t, wuq_t, wukv_t, gains, rope_sw, rope_ml)


def _na_kernel(q_ref, kc_ref, k0_ref, k1_ref, k2_ref, vc_ref, v0_ref, v1_ref, v2_ref, bias_ref, o_ref,
               acc_ref, out_ref):
    tq = q_ref.shape[2]
    zeros = jnp.zeros((HEAD_DIM, tq), bf16)
    k_refs = (kc_ref, k0_ref, k1_ref, k2_ref)
    v_refs = (vc_ref, v0_ref, v1_ref, v2_ref)

    def scores(h, g):
        pair = slice((h // 2) * LANES, (h // 2 + 1) * LANES)
        qh = q_ref[0, h * HEAD_DIM:(h + 1) * HEAD_DIM, :]
        qpad = jnp.concatenate([qh, zeros] if h % 2 == 0 else [zeros, qh], axis=0)
        s = _dot(k_refs[g][0, :, pair], qpad)
        return [s if g == 0 else s + bias_ref[0, h, (g - 1) * TM:g * TM, :]]

    def values(h, g):
        return [v_refs[g][0, 0, h * V_EXT:(h + 1) * V_EXT, :]]

    _blockwise_attention(NA_HEADS, len(k_refs), scores, values, acc_ref, out_ref, o_ref, LOCAL_PIPE)


def _na_operands(q_t, k, v_c, bias, off):
    B, _, Lc = q_t.shape
    T = Lc // TM
    nq = T - off
    nblk = T - 1

    def base(i):
        return 1 + jnp.clip(i + off - 2, 0, nblk - 3)

    def btype(i):
        qt = i + off
        return jnp.where(qt == 0, 3, jnp.where(qt == 1, 0, jnp.where(qt == T - 1, 2, 1)))

    kspec = lambda j: pl.BlockSpec((1, TM, NA_W), lambda b, i: (b, base(i) + j, 0))
    vrows = NA_HEADS * V_EXT
    vspec = lambda j: pl.BlockSpec((1, 1, vrows, TM), lambda b, i: (b, base(i) + j, 0, 0))
    in_specs = [
        pl.BlockSpec((1, NA_W, TM), lambda b, i: (b, 0, i + off)),
        pl.BlockSpec((1, TM, NA_W), lambda b, i: (b, 0, 0)), kspec(0), kspec(1), kspec(2),
        pl.BlockSpec((1, 1, vrows, TM), lambda b, i: (b, 0, 0, 0)), vspec(0), vspec(1), vspec(2),
        pl.BlockSpec((1, NA_HEADS, NA_KEYS, TM), lambda b, i: (btype(i), 0, 0, 0)),
    ]
    return in_specs, (q_t, k, k, k, k, v_c, v_c, v_c, v_c, bias)


def _na_bias_table(rpb, rows):
    col = jnp.arange(GRID_W)
    dc = jnp.clip(col[:, None] - col[None, :], -(NA_WIN_COLS - 1), NA_WIN_COLS - 1) + (NA_WIN_COLS - 1)
    onehot = (dc[None] == jnp.arange(2 * NA_WIN_COLS - 1)[:, None, None]).astype(f32)
    tile = jnp.einsum("hrd,dkq->hrkq", rpb.astype(f32) * LOG2E, onehot, precision=lax.Precision.HIGHEST)
    c0 = jnp.clip(col - NA_WIN_COLS // 2, 0, GRID_W - NA_WIN_COLS)
    in_win = (col[:, None] >= c0[None, :]) & (col[:, None] < c0[None, :] + NA_WIN_COLS)
    tile = jnp.where(in_win, tile, NEG_INF)
    masked = jnp.full((rpb.shape[0], GRID_W, GRID_W), NEG_INF, f32)
    krows = NA_KEYS // GRID_W

    def one(r0, ks):
        out = []
        for kr in range(ks, ks + krows):
            parts = []
            for qr in range(r0, r0 + NA_QROWS):
                start = min(max(qr - NA_WIN_ROWS // 2, 0), rows - NA_WIN_ROWS)
                parts.append(tile[:, kr - qr + NA_WIN_ROWS - 1] if start <= kr < start + NA_WIN_ROWS else masked)
            out.append(jnp.concatenate(parts, axis=2))
        return jnp.concatenate(out, axis=1)

    tabs = [one(0, 0), one(NA_QROWS, 0), one(rows - NA_QROWS, rows - krows)]
    tabs.append(jnp.full_like(tabs[0], NEG_INF))
    return jnp.stack(tabs)


def _sw_kernel(q_ref, kc_ref, k0_ref, k1_ref, k2_ref, k3_ref, vc_ref, v0_ref, v1_ref, v2_ref, v3_ref,
               sink_ref, o_ref, acc_ref, out_ref, *, off, ctx_len, total_len):
    tq = q_ref.shape[2]
    qt = pl.program_id(1) + off
    group = SW_Q_HEADS // SW_KV_HEADS
    k_refs = (k0_ref, k1_ref, k2_ref, k3_ref)
    v_refs = (v0_ref, v1_ref, v2_ref, v3_ref)
    kk = lax.broadcasted_iota(jnp.int32, (LANES, tq), 0)
    qpos = qt * tq + lax.broadcasted_iota(jnp.int32, (LANES, tq), 1)
    madd = []
    for i in range(4):
        kpos = (qt * (tq // LANES) - 1 + i) * LANES + kk
        ok = (kpos >= ctx_len) & (kpos < total_len) & (jnp.abs(kpos - qpos) <= SW_WINDOW) & (qt > 0)
        madd.append(jnp.where(ok, 0.0, NEG_INF).astype(f32))
    zeros = jnp.zeros((HEAD_DIM, tq), bf16)

    nctx = kc_ref.shape[1] // LANES
    pairs = len(k_refs) // 2

    def scores(h, g):
        qh = q_ref[0, h * HEAD_DIM:(h + 1) * HEAD_DIM, :]
        qpad = jnp.concatenate([qh, zeros] if h // group == 0 else [zeros, qh], axis=0)
        if g == 0:
            return [_dot(kc_ref[0, j * LANES:(j + 1) * LANES, :], qpad) for j in range(nctx)]
        return [_dot(k_refs[i][0], qpad) + madd[i] for i in (2 * g - 2, 2 * g - 1)]

    def values(h, g):
        kv_rows = slice((h // group) * V_EXT, (h // group + 1) * V_EXT)
        if g == 0:
            return [vc_ref[0, j, kv_rows, :] for j in range(nctx)]
        return [v_refs[i][0, 0, kv_rows, :] for i in (2 * g - 2, 2 * g - 1)]

    _blockwise_attention(SW_Q_HEADS, 1 + pairs, scores, values, acc_ref, out_ref, o_ref, LOCAL_PIPE,
                         sink=lambda h: sink_ref[h])


def _sw_operands(q_t, k, v_c, sink_rows, off, ctx_len):
    B, _, Lc = q_t.shape
    T = Lc // TM
    nq = T - off
    per = TM // LANES
    lo, hi = ctx_len // LANES, Lc // LANES - 1

    def blk(i, j):
        return jnp.clip((i + off) * per - 1 + j, lo, hi)

    kspec = lambda j: pl.BlockSpec((1, LANES, SW_KV_W), lambda b, i: (b, blk(i, j), 0))
    vrows = SW_KV_HEADS * V_EXT
    vspec = lambda j: pl.BlockSpec((1, 1, vrows, LANES), lambda b, i: (b, blk(i, j), 0, 0))
    in_specs = [
        pl.BlockSpec((1, SW_Q_W, TM), lambda b, i: (b, 0, i + off)),
        pl.BlockSpec((1, ctx_len, SW_KV_W), lambda b, i: (b, 0, 0)),
        kspec(0), kspec(1), kspec(2), kspec(3),
        pl.BlockSpec((1, ctx_len // LANES, vrows, LANES), lambda b, i: (b, 0, 0, 0)),
        vspec(0), vspec(1), vspec(2), vspec(3),
        pl.BlockSpec(sink_rows.shape, lambda b, i: (0, 0, 0)),
    ]
    return in_specs, (q_t, k, k, k, k, k, v_c, v_c, v_c, v_c, v_c, sink_rows)


def _local_kernel(*refs, n_na, n_sw, off, ctx_len, total_len):
    na_in, sw_in = refs[:n_na], refs[n_na:n_na + n_sw]
    o_na, o_sw, acc_na, out_na, acc_sw, out_sw = refs[n_na + n_sw:]
    _na_kernel(*na_in, o_na, acc_na, out_na)
    _sw_kernel(*sw_in, o_sw, acc_sw, out_sw, off=off, ctx_len=ctx_len, total_len=total_len)


def _local(na_q, na_k, na_v, bias, sw_q, sw_k, sw_v, sink_rows, off, ctx_len):
    B, _, Lc = na_q.shape
    nq = Lc // TM - off
    na_specs, na_ops = _na_operands(na_q, na_k, na_v, bias, off)
    sw_specs, sw_ops = _sw_operands(sw_q, sw_k, sw_v, sink_rows, off, ctx_len)
    ospec = pl.BlockSpec((1, TM, BRANCH_W), lambda b, i: (b, i, 0))
    oshape = jax.ShapeDtypeStruct((B, nq * TM, BRANCH_W), bf16)
    scratch = [pltpu.VMEM((BRANCH_W // HEAD_DIM * V_EXT, TM), f32), pltpu.VMEM((BRANCH_W, TM), f32)]
    return pl.pallas_call(
        functools.partial(_local_kernel, n_na=len(na_ops), n_sw=len(sw_ops), off=off, ctx_len=ctx_len,
                          total_len=Lc),
        grid=(B, nq),
        in_specs=na_specs + sw_specs,
        out_specs=[ospec, ospec],
        out_shape=[oshape, oshape],
        scratch_shapes=scratch + scratch,
        compiler_params=_cparams("parallel", "arbitrary"),
        name="local_attn",
    )(*na_ops, *sw_ops)


def _mla_kernel(q_ref, k_ref, v_ref, o_ref, acc_ref, out_ref, *, off):
    nchunk = v_ref.shape[1]
    tq = q_ref.shape[2]

    def attend(chunks):
        acc_ref[...] = jnp.zeros_like(acc_ref)
        ms = [jnp.full((1, tq), NEG_INF, f32) for _ in range(MLA_HEADS)]

        def scores(u):
            h, c = u % MLA_HEADS, chunks[u // MLA_HEADS]
            pair = slice((h // 2) * 2 * MLA_PAD, (h // 2 + 1) * 2 * MLA_PAD)
            qh = q_ref[0, h * MLA_PAD:(h + 1) * MLA_PAD, :]
            zeros = jnp.zeros_like(qh)
            qpad = jnp.concatenate([qh, zeros] if h % 2 == 0 else [zeros, qh], axis=0)
            return _dot(k_ref[0, c * TM:(c + 1) * TM, pair], qpad)

        def softmax(u, s):
            h = u % MLA_HEADS
            m_old = ms[h]
            ms[h] = jnp.maximum(m_old, jnp.max(s, axis=0, keepdims=True))
            return jnp.exp2(m_old - ms[h]), jnp.exp2(s - ms[h]).astype(bf16)

        def weighted_values(u, item):
            h, c = u % MLA_HEADS, chunks[u // MLA_HEADS]
            alpha, p = item
            rows = slice(h * V_EXT, (h + 1) * V_EXT)
            acc_ref[rows, :] = alpha * acc_ref[rows, :] + _dot(v_ref[0, c, rows, :], p)

        _pipelined_heads(MLA_HEADS * len(chunks), scores, softmax, weighted_values, *MLA_PIPE)
        for h in range(MLA_HEADS):
            num = acc_ref[h * V_EXT:h * V_EXT + MLA_V, :]
            out_ref[h * MLA_V:(h + 1) * MLA_V, :] = num / acc_ref[h * V_EXT + MLA_V:h * V_EXT + MLA_V + 1, :]
        o_ref[0] = out_ref[...].T.astype(bf16)

    if off == 0:
        qt = pl.program_id(1)
        pl.when(qt == 0)(lambda: attend([0]))
        pl.when(qt != 0)(lambda: attend(list(range(nchunk))))
    else:
        attend(list(range(nchunk)))


def _mla(q_t, k, v_c, off):
    B, _, Lc = q_t.shape
    T = Lc // TM
    nq = T - off
    return pl.pallas_call(
        functools.partial(_mla_kernel, off=off),
        grid=(B, nq),
        in_specs=[
            pl.BlockSpec((1, MLA_HEADS * MLA_PAD, TM), lambda b, i: (b, 0, i + off)),
            pl.BlockSpec((1, Lc, MLA_HEADS * MLA_PAD), lambda b, i: (b, 0, 0)),
            pl.BlockSpec((1, T, MLA_HEADS * V_EXT, TM), lambda b, i: (b, 0, 0, 0)),
        ],
        out_specs=pl.BlockSpec((1, TM, BRANCH_W), lambda b, i: (b, i, 0)),
        out_shape=jax.ShapeDtypeStruct((B, nq * TM, BRANCH_W), bf16),
        scratch_shapes=[pltpu.VMEM((MLA_HEADS * V_EXT, TM), f32), pltpu.VMEM((BRANCH_W, TM), f32)],
        compiler_params=_cparams("parallel", "arbitrary"),
        name="mla_attn",
    )(q_t, k, v_c)


def _merge_kernel(c_ref, x_ref, a_ref, s_ref, ga_ref, ona_ref, osw_ref, oml_ref, wg_ref, wb_ref, wo_ref, o_ref,
                  *, off):
    x = _stream_tile(c_ref, x_ref, pl.program_id(1) + off)
    d = x.shape[1]
    hb = _norm_mod(x, a_ref[0], s_ref[0]).astype(bf16)
    z = None
    for n, o_n in enumerate((ona_ref, osw_ref, oml_ref)):
        g = jax.nn.sigmoid(_dot(hb, wg_ref[:, n * d:(n + 1) * d]))
        y = g * _dot(o_n[0], wb_ref[n])
        z = y if z is None else z + y
    o_ref[0] = x + ga_ref[0] * _dot(z.astype(bf16), wo_ref[...])


def _merge(c_src, x_src, a_row, s_row, g_row, o_na, o_sw, o_ml, w_gate, w_branch, w_out, off):
    B, _, D = x_src.shape
    nt = x_src.shape[1] // TM + (0 if c_src is x_src else 1) - off
    nb = a_row.shape[0] - 1
    row = lambda b, i: (jnp.where(i + off == 0, nb, b), 0, 0)
    ospec = pl.BlockSpec((1, TM, BRANCH_W), lambda b, i: (b, i, 0))
    return pl.pallas_call(
        functools.partial(_merge_kernel, off=off),
        grid=(B, nt),
        in_specs=[
            *_stream_specs(c_src, x_src, off),
            pl.BlockSpec((1, 1, D), row), pl.BlockSpec((1, 1, D), row), pl.BlockSpec((1, 1, D), row),
            ospec, ospec, ospec,
            _const_spec(w_gate.shape), _const_spec(w_branch.shape), _const_spec(w_out.shape),
        ],
        out_specs=pl.BlockSpec((1, TM, D), lambda b, i: (b, i, 0)),
        out_shape=jax.ShapeDtypeStruct((B, nt * TM, D), f32),
        compiler_params=_cparams("parallel", "parallel"),
        name="merge",
    )(c_src, x_src, a_row, s_row, g_row, o_na, o_sw, o_ml, w_gate, w_branch, w_out)


def _ffn_kernel(x_ref, xp_ref, xn_ref, a_ref, s_ref, gf_ref, wup_ref, cw_ref, cb_ref, wdn_ref, o_ref, u_ref,
                *, ctx_tiles):
    tm = x_ref.shape[1]
    halo = xp_ref.shape[1]
    nc = wup_ref.shape[0]
    t = pl.program_id(1)
    nt = pl.num_programs(1)
    x = x_ref[0]
    xe = jnp.concatenate([xp_ref[0], x, xn_ref[0]], axis=0)
    hb = _norm_mod(xe, a_ref[0], s_ref[0]).astype(bf16)
    left_ok = jnp.where(jnp.logical_and(t != 0, t != ctx_tiles), 1.0, 0.0)
    right_ok = jnp.where(jnp.logical_and(t != nt - 1, t != ctx_tiles - 1), 1.0, 0.0)

    def up(c):
        u = _dot(hb, wup_ref[c])
        slot = u_ref.at[c % 2]
        slot[:halo] = u[:halo] * left_ok
        slot[halo:halo + tm] = u[halo:halo + tm]
        slot[halo + tm:] = u[halo + tm:] * right_ok

    def conv_act(c):
        slot = u_ref.at[c % 2]
        w = cw_ref[c]
        uc = (w[0:1] * slot[halo - 1:halo - 1 + tm] + w[1:2] * slot[halo:halo + tm]
              + w[2:3] * slot[halo + 1:halo + 1 + tm] + cb_ref[c])
        g, v = uc[:, :FF_CHUNK], uc[:, FF_CHUNK:]
        return (g * jax.nn.sigmoid(g) * v).astype(bf16)

    acc = jnp.zeros(x.shape, f32)
    pending = []
    up(0)
    for c in range(nc):
        if c + 1 < nc:
            up(c + 1)
        pending.append((c, conv_act(c)))
        if len(pending) > FF_DOWN_BEHIND:
            j, act = pending.pop(0)
            acc = acc + _dot(act, wdn_ref[j])
    for j, act in pending:
        acc = acc + _dot(act, wdn_ref[j])
    o_ref[0] = x + gf_ref[0] * acc


def _ffn(x1, a_row, s_row, g_row, w_up_c, conv_w_c, conv_b_c, w_down_c, ctx_tiles):
    B, Lt, D = x1.shape
    nt = Lt // TM
    nb = a_row.shape[0] - 1
    per = TM // SUBLANES
    last = Lt // SUBLANES - 1
    row = lambda b, t: (jnp.where(t < ctx_tiles, nb, b), 0, 0)
    return pl.pallas_call(
        functools.partial(_ffn_kernel, ctx_tiles=ctx_tiles),
        grid=(B, nt),
        in_specs=[
            pl.BlockSpec((1, TM, D), lambda b, t: (b, t, 0)),
            pl.BlockSpec((1, SUBLANES, D), lambda b, t: (b, jnp.maximum(t * per - 1, 0), 0)),
            pl.BlockSpec((1, SUBLANES, D), lambda b, t: (b, jnp.minimum((t + 1) * per, last), 0)),
            pl.BlockSpec((1, 1, D), row), pl.BlockSpec((1, 1, D), row), pl.BlockSpec((1, 1, D), row),
            _const_spec(w_up_c.shape), _const_spec(conv_w_c.shape), _const_spec(conv_b_c.shape),
            _const_spec(w_down_c.shape),
        ],
        out_specs=pl.BlockSpec((1, TM, D), lambda b, t: (b, t, 0)),
        out_shape=jax.ShapeDtypeStruct((B, Lt, D), f32),
        scratch_shapes=[pltpu.VMEM((2, TM + 2 * SUBLANES, 2 * FF_CHUNK), f32)],
        compiler_params=_cparams("parallel", "arbitrary"),
        name="ffn",
    )(x1, x1, x1, a_row, s_row, g_row, w_up_c, conv_w_c, conv_b_c, w_down_c)


def _rope_table(seq, ctx_len, n):
    t = jnp.arange(seq)
    inv = ROPE_BASE ** (-jnp.arange(n, dtype=f32) / n)
    ar = (t // GRID_W).astype(f32)[None, :] * inv[:, None]
    ac = (t % GRID_W).astype(f32)[None, :] * inv[:, None]
    lat = jnp.concatenate([jnp.cos(ar), jnp.sin(ar), jnp.cos(ac), jnp.sin(ac)], axis=0)
    one, zero = jnp.ones((n, ctx_len), f32), jnp.zeros((n, ctx_len), f32)
    return jnp.concatenate([jnp.concatenate([one, zero, one, zero], axis=0), lat], axis=1)


def _chunked_ffn_weights(w_up, conv_w, conv_b, w_down):
    d_ff = w_down.shape[0]
    nc = d_ff // FF_CHUNK
    pair = lambda a: jnp.concatenate([a[..., :d_ff].reshape(a.shape[:-1] + (nc, FF_CHUNK)),
                                      a[..., d_ff:].reshape(a.shape[:-1] + (nc, FF_CHUNK))], axis=-1)
    w_up_c = jnp.moveaxis(pair(w_up), 1, 0).astype(bf16)
    conv_w_c = jnp.moveaxis(pair(conv_w), 1, 0)
    conv_b_c = pair(conv_b)[:, None, :]
    w_down_c = w_down.reshape(nc, FF_CHUNK, w_down.shape[1]).astype(bf16)
    return w_up_c, conv_w_c, conv_b_c, w_down_c


def kernel(x, c, ctx, c_ctx, w_ada, b_ada, g_mix, g_ffn, w_in, na_q_norm, na_k_norm, na_rpb, sw_q_norm, sw_k_norm,
           sw_sink, mla_q_rank_norm, mla_kv_rank_norm, w_uq, w_ukv, mla_q_norm, mla_k_norm, w_branch, w_out,
           w_up, conv_w, conv_b, w_down):
    B, S, D = x.shape
    C = ctx.shape[1]
    depth = w_ada.shape[0]
    assert C == TM and S % TM == 0 and GRID_W * NA_QROWS == TM and (S // GRID_W) * GRID_W == S
    assert S // GRID_W >= NA_KEYS // GRID_W and w_down.shape[1] % FF_CHUNK == 0
    rows = S // GRID_W

    c_src, x_src = ctx, x
    n_mod = -(-(B + 1) // SUBLANES) * SUBLANES
    c_all = jnp.zeros((n_mod, D), f32).at[:B].set(c).at[B].set(c_ctx)
    zero = jnp.zeros_like(g_mix)
    g_sel = jnp.stack([zero, g_mix, zero, zero, g_ffn, zero], axis=1)[:, :, None, :]
    mod = _ada(c_all, w_ada, b_ada, g_sel)

    rope_sw = _rope_table(S, C, HEAD_DIM // 4)
    rope_ml = _rope_table(S, C, MLA_ROPE // 4)
    group = SW_Q_HEADS // SW_KV_HEADS

    for l in range(depth):
        off = 0 if l < depth - 1 else 1
        m6 = mod[l, :B + 1].reshape(B + 1, 6, 1, D)
        shift_a, a_mix, gate_a, shift_f, a_ffn, gate_f = (m6[:, j] for j in range(6))
        w_in_t = w_in[l][:, :N_STREAM].T.astype(bf16)
        gains = jnp.concatenate([na_q_norm[l], na_k_norm[l], sw_q_norm[l], sw_k_norm[l], mla_q_rank_norm[l],
                                 mla_kv_rank_norm[l], mla_q_norm[l], mla_k_norm[l]])[:, None]
        (na_q, na_k, na_v, sw_q, sw_k, sw_v, m_q, m_k, m_v) = _proj(
            c_src, x_src, a_mix, shift_a, w_in_t, w_uq[l].T.astype(bf16), w_ukv[l].T.astype(bf16),
            gains, rope_sw, rope_ml)
        sink_rows = jnp.broadcast_to((sw_sink[l] * LOG2E)[:, None, None], (SW_Q_HEADS, 1, TM))
        o_na, o_sw = _local(na_q, na_k, na_v, _na_bias_table(na_rpb[l], rows), sw_q, sw_k, sw_v, sink_rows, off, C)
        o_ml = _mla(m_q, m_k, m_v, off)
        x1 = _merge(c_src, x_src, a_mix, shift_a, gate_a, o_na, o_sw, o_ml, w_in[l][:, N_STREAM:].astype(bf16),
                    w_branch[l].astype(bf16), w_out[l].astype(bf16), off)
        xc = _ffn(x1, a_ffn, shift_f, gate_f, *_chunked_ffn_weights(w_up[l], conv_w[l], conv_b[l], w_down[l]),
                  ctx_tiles=1 - off)
        c_src = x_src = xc
    return xc
```

```python
import functools

import jax
import jax.numpy as jnp
from jax import lax
from jax.experimental import pallas as pl
from jax.experimental.pallas import tpu as pltpu

f32 = jnp.float32
bf16 = jnp.bfloat16

GRID_W = 64
HEAD_DIM = 64
ROPE_BASE = 10000.0
NORM_EPS = 1e-6
NEG_INF = -1e30
NA_HEADS = 8
NA_WIN_ROWS = 8
NA_WIN_COLS = 16
SW_Q_HEADS = 8
SW_KV_HEADS = 2
SW_WINDOW = 128
MLA_HEADS = 8
MLA_Q_RANK = 384
MLA_KV_RANK = 256
MLA_NOPE = 64
MLA_ROPE = 32
MLA_V = 64
N_BRANCH = 3
BRANCH_W = 512
CONV_W = 3

LANES = 128
SUBLANES = 8
TM = 256
MLA_PAD = 128
LOCAL_PIPE = (4, 3)
MLA_PIPE = (4, 3)
FF_CHUNK = 256
FF_DOWN_BEHIND = 2
NA_QROWS = TM // GRID_W
NA_KEYS = 3 * TM
BF16_ROWS = 16
V_EXT = HEAD_DIM + BF16_ROWS
LOG2E = 1.4426950408889634
VMEM_LIMIT = 56 * 1024 * 1024

_SEG = {}
_o = 0
NA_W = NA_HEADS * HEAD_DIM
SW_Q_W = SW_Q_HEADS * HEAD_DIM
SW_KV_W = SW_KV_HEADS * HEAD_DIM
MLA_W = MLA_HEADS * MLA_PAD
for _name, _n in (("na_q", NA_W), ("na_k", NA_W), ("na_v", NA_W), ("sw_q", SW_Q_W), ("sw_k", SW_KV_W),
                  ("sw_v", SW_KV_W), ("c_q", MLA_Q_RANK), ("c_kv", MLA_KV_RANK), ("k_r", MLA_ROPE)):
    _SEG[_name] = (_o, _o + _n)
    _o += _n
N_STREAM = _o
_GN = {}
_o = 0
for _name, _n in (("na_q", 64), ("na_k", 64), ("sw_q", 64), ("sw_k", 64), ("mla_qr", MLA_Q_RANK),
                  ("mla_kvr", MLA_KV_RANK), ("mla_q", 96), ("mla_k", 96)):
    _GN[_name] = (_o, _o + _n)
    _o += _n
N_GAIN = _o


def _dot(a, b):
    return jnp.dot(a, b, preferred_element_type=f32)


def _cparams(*sem):
    return pltpu.CompilerParams(dimension_semantics=sem, vmem_limit_bytes=VMEM_LIMIT)


def _const_spec(shape):
    nd = len(shape)
    return pl.BlockSpec(shape, lambda *_: (0,) * nd, pipeline_mode=pl.Buffered(1))


def _ada_kernel(c_ref, w_ref, b_ref, g_ref, o_ref):
    n = pl.program_id(1)
    c = c_ref[...]
    sc = (c * jax.nn.sigmoid(c)).astype(bf16)
    mod = _dot(sc, w_ref[0].astype(bf16)) + b_ref[0]
    fold = jnp.logical_or(n == 1, n == 4)
    o_ref[0] = jnp.where(fold, g_ref[0, 0] * (1.0 + mod), mod)


def _ada(c_all, w_ada, b_ada, g_sel):
    L, D, _ = w_ada.shape
    R = c_all.shape[0]
    return pl.pallas_call(
        _ada_kernel,
        grid=(L, 6),
        in_specs=[
            pl.BlockSpec((R, D), lambda l, n: (0, 0)),
            pl.BlockSpec((1, D, D), lambda l, n: (l, 0, n)),
            pl.BlockSpec((1, 1, D), lambda l, n: (l, 0, n)),
            pl.BlockSpec((1, 1, 1, D), lambda l, n: (l, n, 0, 0)),
        ],
        out_specs=pl.BlockSpec((1, R, D), lambda l, n: (l, 0, n)),
        out_shape=jax.ShapeDtypeStruct((L, R, 6 * D), f32),
        compiler_params=_cparams("parallel", "parallel"),
        name="ada",
    )(c_all, w_ada, b_ada.reshape(L, 1, 6 * D), g_sel)


def _norm_mod(x, a, s):
    ms = jnp.mean(x * x, axis=-1, keepdims=True)
    return (x * lax.rsqrt(ms + NORM_EPS)) * a + s


def _head_norm(y3, gain):
    ms = jnp.mean(y3 * y3, axis=1, keepdims=True)
    return y3 * lax.rsqrt(ms + NORM_EPS) * gain[None]


def _rope(t3, tab, lo, n):
    cr, sr, cc, sc = (tab[i * n:(i + 1) * n][None] for i in range(4))
    a, b, c, e = (t3[:, lo + i * n:lo + (i + 1) * n] for i in range(4))
    return [a * cr - b * sr, a * sr + b * cr, c * cc - e * sc, c * sc + e * cc]


def _pipelined_heads(n, scores, softmax, weighted_values, ahead, behind):
    s, pending = {}, []
    for h in range(min(ahead, n)):
        s[h] = scores(h)
    for h in range(n):
        if h + ahead < n:
            s[h + ahead] = scores(h + ahead)
        pending.append((h, softmax(h, s.pop(h))))
        if len(pending) > behind:
            weighted_values(*pending.pop(0))
    for item in pending:
        weighted_values(*item)


def _blockwise_attention(heads, groups, scores, values, acc_ref, out_ref, o_ref, pipe, sink=None):
    tq = acc_ref.shape[1]
    acc_ref[...] = jnp.zeros_like(acc_ref)
    ms = [jnp.full((1, tq), NEG_INF, f32) for _ in range(heads)]

    def softmax(u, s):
        h = u % heads
        m_old = ms[h]
        ms[h] = functools.reduce(jnp.maximum, [jnp.max(x, axis=0, keepdims=True) for x in s] + [m_old])
        return jnp.exp2(m_old - ms[h]), [jnp.exp2((x - ms[h]).astype(bf16)) for x in s]

    def weighted_values(u, item):
        h, g = u % heads, u // heads
        alpha, p = item
        rows = slice(h * V_EXT, (h + 1) * V_EXT)
        upd = functools.reduce(jnp.add, [_dot(v, x) for v, x in zip(values(h, g), p)])
        acc_ref[rows, :] = alpha * acc_ref[rows, :] + upd

    _pipelined_heads(heads * groups, lambda u: scores(u % heads, u // heads), softmax, weighted_values, *pipe)
    for h in range(heads):
        den = acc_ref[h * V_EXT + HEAD_DIM:h * V_EXT + HEAD_DIM + 1, :]
        if sink is not None:
            den = den + jnp.exp2(sink(h) - ms[h])
        out_ref[h * HEAD_DIM:(h + 1) * HEAD_DIM, :] = acc_ref[h * V_EXT:h * V_EXT + HEAD_DIM, :] / den
    o_ref[0] = out_ref[...].T.astype(bf16)


def _stream_tile(c_ref, x_ref, t):
    return jnp.where(t == 0, c_ref[0], x_ref[0])


def _stream_specs(c_src, x_src, off):
    shift = 0 if c_src is x_src else 1
    d = x_src.shape[2]
    return [pl.BlockSpec((1, TM, d), lambda b, i: (b, 0, 0)),
            pl.BlockSpec((1, TM, d), lambda b, i: (b, jnp.maximum(i + off - shift, 0), 0))]


def _proj_kernel(c_ref, x_ref, a_ref, s_ref, win_ref, wuq_ref, wukv_ref, gn_ref, rsw_ref, rml_ref,
                 naq_ref, nak_ref, nav_ref, swq_ref, swk_ref, swv_ref, mq_ref, mk_ref, mv_ref):
    tm = x_ref.shape[1]
    h = _norm_mod(_stream_tile(c_ref, x_ref, pl.program_id(1)), a_ref[0], s_ref[0])
    ht = h.T.astype(bf16)

    def seg(name):
        lo, hi = _SEG[name]
        return _dot(win_ref[lo:hi, :], ht)

    def gain(name):
        lo, hi = _GN[name]
        return gn_ref[lo:hi, :]

    def with_ones(v, heads):
        v3 = v.reshape(heads, HEAD_DIM, tm)
        ones = jnp.ones((heads, BF16_ROWS, tm), f32)
        return jnp.concatenate([v3, ones], axis=1).reshape(heads * V_EXT, tm).astype(bf16)

    def rank_norm(y, name):
        return (y * lax.rsqrt(jnp.mean(y * y, axis=0, keepdims=True) + NORM_EPS) * gain(name)).astype(bf16)

    y_cq, y_ckv, kr = seg("c_q"), seg("c_kv"), seg("k_r")
    y_naq, y_nak = seg("na_q"), seg("na_k")
    mq = _dot(wuq_ref[...], rank_norm(y_cq, "mla_qr"))
    kv = _dot(wukv_ref[...], rank_norm(y_ckv, "mla_kvr"))
    y_nav, y_swq, y_swk, y_swv = seg("na_v"), seg("sw_q"), seg("sw_k"), seg("sw_v")

    qs = HEAD_DIM ** -0.5 * LOG2E
    q = _head_norm(y_naq.reshape(NA_HEADS, HEAD_DIM, tm), gain("na_q")) * qs
    naq_ref[0] = q.reshape(NA_HEADS * HEAD_DIM, tm).astype(bf16)
    k = _head_norm(y_nak.reshape(NA_HEADS, HEAD_DIM, tm), gain("na_k"))
    nak_ref[0] = k.reshape(NA_HEADS * HEAD_DIM, tm).T.astype(bf16)
    nav_ref[0, 0] = with_ones(y_nav, NA_HEADS)

    rml = rml_ref[...]
    nr = MLA_ROPE // 4
    dq = MLA_NOPE + MLA_ROPE
    pad = jnp.zeros((MLA_HEADS, MLA_PAD - dq, tm), f32)
    mq = _head_norm(mq.reshape(MLA_HEADS, dq, tm), gain("mla_q"))
    mq = jnp.concatenate([mq[:, :MLA_NOPE]] + _rope(mq, rml, MLA_NOPE, nr), axis=1) * (dq ** -0.5 * LOG2E)
    mq_ref[0] = jnp.concatenate([mq, pad], axis=1).reshape(MLA_HEADS * MLA_PAD, tm).astype(bf16)

    kv = kv.reshape(MLA_HEADS, MLA_NOPE + MLA_V, tm)
    kn, v = kv[:, :MLA_NOPE], kv[:, MLA_NOPE:]
    ss = jnp.sum(kn * kn, axis=1, keepdims=True) + jnp.sum(kr * kr, axis=0, keepdims=True)[None]
    r = lax.rsqrt(ss / dq + NORM_EPS)
    gk = gain("mla_k")
    kn = kn * r * gk[:MLA_NOPE][None]
    krh = kr[None] * r * gk[MLA_NOPE:][None]
    mk = jnp.concatenate([kn] + _rope(krh, rml, 0, nr) + [pad], axis=1)
    mk_ref[0] = mk.reshape(MLA_HEADS * MLA_PAD, tm).T.astype(bf16)
    mv_ref[0, 0] = with_ones(v.reshape(MLA_HEADS * MLA_V, tm), MLA_HEADS)

    rsw = rsw_ref[...]
    nq = HEAD_DIM // 4
    q = _head_norm(y_swq.reshape(SW_Q_HEADS, HEAD_DIM, tm), gain("sw_q"))
    q = jnp.concatenate(_rope(q, rsw, 0, nq), axis=1) * qs
    swq_ref[0] = q.reshape(SW_Q_HEADS * HEAD_DIM, tm).astype(bf16)
    k = _head_norm(y_swk.reshape(SW_KV_HEADS, HEAD_DIM, tm), gain("sw_k"))
    k = jnp.concatenate(_rope(k, rsw, 0, nq), axis=1)
    swk_ref[0] = k.reshape(SW_KV_HEADS * HEAD_DIM, tm).T.astype(bf16)
    v = with_ones(y_swv, SW_KV_HEADS)
    for j in range(tm // LANES):
        swv_ref[0, j] = v[:, j * LANES:(j + 1) * LANES]


def _proj(c_src, x_src, a_row, s_row, win_t, wuq_t, wukv_t, gains, rope_sw, rope_ml):
    B, _, D = x_src.shape
    T = x_src.shape[1] // TM + (0 if c_src is x_src else 1)
    Lc = T * TM
    nb = a_row.shape[0] - 1
    row = lambda b, t: (jnp.where(t == 0, nb, b), 0, 0)
    fm = lambda n: pl.BlockSpec((1, n, TM), lambda b, t: (b, 0, t))
    tk = lambda n: pl.BlockSpec((1, TM, n), lambda b, t: (b, t, 0))
    sds = jax.ShapeDtypeStruct
    return pl.pallas_call(
        _proj_kernel,
        grid=(B, T),
        in_specs=[
            *_stream_specs(c_src, x_src, 0),
            pl.BlockSpec((1, 1, D), row),
            pl.BlockSpec((1, 1, D), row),
            _const_spec(win_t.shape), _const_spec(wuq_t.shape), _const_spec(wukv_t.shape),
            _const_spec(gains.shape),
            pl.BlockSpec((rope_sw.shape[0], TM), lambda b, t: (0, t)),
            pl.BlockSpec((rope_ml.shape[0], TM), lambda b, t: (0, t)),
        ],
        out_specs=[
            fm(NA_W), tk(NA_W), pl.BlockSpec((1, 1, NA_HEADS * V_EXT, TM), lambda b, t: (b, t, 0, 0)),
            fm(SW_Q_W), tk(SW_KV_W),
            pl.BlockSpec((1, TM // LANES, SW_KV_HEADS * V_EXT, LANES), lambda b, t: (b, t, 0, 0)),
            fm(MLA_W), tk(MLA_W),
            pl.BlockSpec((1, 1, MLA_HEADS * V_EXT, TM), lambda b, t: (b, t, 0, 0)),
        ],
        out_shape=[
            sds((B, NA_W, Lc), bf16), sds((B, Lc, NA_W), bf16), sds((B, T, NA_HEADS * V_EXT, TM), bf16),
            sds((B, SW_Q_W, Lc), bf16), sds((B, Lc, SW_KV_W), bf16),
            sds((B, Lc // LANES, SW_KV_HEADS * V_EXT, LANES), bf16),
            sds((B, MLA_W, Lc), bf16), sds((B, Lc, MLA_W), bf16),
            sds((B, T, MLA_HEADS * V_EXT, TM), bf16),
        ],
        compiler_params=_cparams("parallel", "parallel"),
        name="proj",
    )(c_src, x_src, a_row, s_row, win_t, wuq_t, wukv_t, gains, rope_sw, rope_ml)


def _na_kernel(q_ref, kc_ref, k0_ref, k1_ref, k2_ref, vc_ref, v0_ref, v1_ref, v2_ref, bias_ref, o_ref,
               acc_ref, out_ref):
    tq = q_ref.shape[2]
    zeros = jnp.zeros((HEAD_DIM, tq), bf16)
    k_refs = (kc_ref, k0_ref, k1_ref, k2_ref)
    v_refs = (vc_ref, v0_ref, v1_ref, v2_ref)

    def scores(h, g):
        pair = slice((h // 2) * LANES, (h // 2 + 1) * LANES)
        qh = q_ref[0, h * HEAD_DIM:(h + 1) * HEAD_DIM, :]
        qpad = jnp.concatenate([qh, zeros] if h % 2 == 0 else [zeros, qh], axis=0)
        s = _dot(k_refs[g][0, :, pair], qpad)
        return [s if g == 0 else s + bias_ref[0, h, (g - 1) * TM:g * TM, :]]

    def values(h, g):
        return [v_refs[g][0, 0, h * V_EXT:(h + 1) * V_EXT, :]]

    _blockwise_attention(NA_HEADS, len(k_refs), scores, values, acc_ref, out_ref, o_ref, LOCAL_PIPE)


def _na_operands(q_t, k, v_c, bias, off):
    B, _, Lc = q_t.shape
    T = Lc // TM
    nq = T - off
    nblk = T - 1

    def base(i):
        return 1 + jnp.clip(i + off - 2, 0, nblk - 3)

    def btype(i):
        qt = i + off
        return jnp.where(qt == 0, 3, jnp.where(qt == 1, 0, jnp.where(qt == T - 1, 2, 1)))

    kspec = lambda j: pl.BlockSpec((1, TM, NA_W), lambda b, i: (b, base(i) + j, 0))
    vrows = NA_HEADS * V_EXT
    vspec = lambda j: pl.BlockSpec((1, 1, vrows, TM), lambda b, i: (b, base(i) + j, 0, 0))
    in_specs = [
        pl.BlockSpec((1, NA_W, TM), lambda b, i: (b, 0, i + off)),
        pl.BlockSpec((1, TM, NA_W), lambda b, i: (b, 0, 0)), kspec(0), kspec(1), kspec(2),
        pl.BlockSpec((1, 1, vrows, TM), lambda b, i: (b, 0, 0, 0)), vspec(0), vspec(1), vspec(2),
        pl.BlockSpec((1, NA_HEADS, NA_KEYS, TM), lambda b, i: (btype(i), 0, 0, 0)),
    ]
    return in_specs, (q_t, k, k, k, k, v_c, v_c, v_c, v_c, bias)


def _na_bias_table(rpb, rows):
    col = jnp.arange(GRID_W)
    dc = jnp.clip(col[:, None] - col[None, :], -(NA_WIN_COLS - 1), NA_WIN_COLS - 1) + (NA_WIN_COLS - 1)
    onehot = (dc[None] == jnp.arange(2 * NA_WIN_COLS - 1)[:, None, None]).astype(f32)
    tile = jnp.einsum("hrd,dkq->hrkq", rpb.astype(f32) * LOG2E, onehot, precision=lax.Precision.HIGHEST)
    c0 = jnp.clip(col - NA_WIN_COLS // 2, 0, GRID_W - NA_WIN_COLS)
    in_win = (col[:, None] >= c0[None, :]) & (col[:, None] < c0[None, :] + NA_WIN_COLS)
    tile = jnp.where(in_win, tile, NEG_INF)
    masked = jnp.full((rpb.shape[0], GRID_W, GRID_W), NEG_INF, f32)
    krows = NA_KEYS // GRID_W

    def one(r0, ks):
        out = []
        for kr in range(ks, ks + krows):
            parts = []
            for qr in range(r0, r0 + NA_QROWS):
                start = min(max(qr - NA_WIN_ROWS // 2, 0), rows - NA_WIN_ROWS)
                parts.append(tile[:, kr - qr + NA_WIN_ROWS - 1] if start <= kr < start + NA_WIN_ROWS else masked)
            out.append(jnp.concatenate(parts, axis=2))
        return jnp.concatenate(out, axis=1)

    tabs = [one(0, 0), one(NA_QROWS, 0), one(rows - NA_QROWS, rows - krows)]
    tabs.append(jnp.full_like(tabs[0], NEG_INF))
    return jnp.stack(tabs)


def _sw_kernel(q_ref, kc_ref, k0_ref, k1_ref, k2_ref, k3_ref, vc_ref, v0_ref, v1_ref, v2_ref, v3_ref,
               sink_ref, o_ref, acc_ref, out_ref, *, off, ctx_len, total_len):
    tq = q_ref.shape[2]
    qt = pl.program_id(1) + off
    group = SW_Q_HEADS // SW_KV_HEADS
    k_refs = (k0_ref, k1_ref, k2_ref, k3_ref)
    v_refs = (v0_ref, v1_ref, v2_ref, v3_ref)
    kk = lax.broadcasted_iota(jnp.int32, (LANES, tq), 0)
    qpos = qt * tq + lax.broadcasted_iota(jnp.int32, (LANES, tq), 1)
    madd = []
    for i in range(4):
        kpos = (qt * (tq // LANES) - 1 + i) * LANES + kk
        ok = (kpos >= ctx_len) & (kpos < total_len) & (jnp.abs(kpos - qpos) <= SW_WINDOW) & (qt > 0)
        madd.append(jnp.where(ok, 0.0, NEG_INF).astype(f32))
    zeros = jnp.zeros((HEAD_DIM, tq), bf16)

    nctx = kc_ref.shape[1] // LANES
    pairs = len(k_refs) // 2

    def scores(h, g):
        qh = q_ref[0, h * HEAD_DIM:(h + 1) * HEAD_DIM, :]
        qpad = jnp.concatenate([qh, zeros] if h // group == 0 else [zeros, qh], axis=0)
        if g == 0:
            return [_dot(kc_ref[0, j * LANES:(j + 1) * LANES, :], qpad) for j in range(nctx)]
        return [_dot(k_refs[i][0], qpad) + madd[i] for i in (2 * g - 2, 2 * g - 1)]

    def values(h, g):
        kv_rows = slice((h // group) * V_EXT, (h // group + 1) * V_EXT)
        if g == 0:
            return [vc_ref[0, j, kv_rows, :] for j in range(nctx)]
        return [v_refs[i][0, 0, kv_rows, :] for i in (2 * g - 2, 2 * g - 1)]

    _blockwise_attention(SW_Q_HEADS, 1 + pairs, scores, values, acc_ref, out_ref, o_ref, LOCAL_PIPE,
                         sink=lambda h: sink_ref[h])


def _sw_operands(q_t, k, v_c, sink_rows, off, ctx_len):
    B, _, Lc = q_t.shape
    T = Lc // TM
    nq = T - off
    per = TM // LANES
    lo, hi = ctx_len // LANES, Lc // LANES - 1

    def blk(i, j):
        return jnp.clip((i + off) * per - 1 + j, lo, hi)

    kspec = lambda j: pl.BlockSpec((1, LANES, SW_KV_W), lambda b, i: (b, blk(i, j), 0))
    vrows = SW_KV_HEADS * V_EXT
    vspec = lambda j: pl.BlockSpec((1, 1, vrows, LANES), lambda b, i: (b, blk(i, j), 0, 0))
    in_specs = [
        pl.BlockSpec((1, SW_Q_W, TM), lambda b, i: (b, 0, i + off)),
        pl.BlockSpec((1, ctx_len, SW_KV_W), lambda b, i: (b, 0, 0)),
        kspec(0), kspec(1), kspec(2), kspec(3),
        pl.BlockSpec((1, ctx_len // LANES, vrows, LANES), lambda b, i: (b, 0, 0, 0)),
        vspec(0), vspec(1), vspec(2), vspec(3),
        pl.BlockSpec(sink_rows.shape, lambda b, i: (0, 0, 0)),
    ]
    return in_specs, (q_t, k, k, k, k, k, v_c, v_c, v_c, v_c, v_c, sink_rows)


def _local_kernel(*refs, n_na, n_sw, off, ctx_len, total_len):
    na_in, sw_in, ml_in = refs[:n_na], refs[n_na:n_na + n_sw], refs[n_na + n_sw:n_na + n_sw + 3]
    o_na, o_sw, o_ml, acc_na, out_na, acc_sw, out_sw, acc_ml, out_ml = refs[n_na + n_sw + 3:]
    _na_kernel(*na_in, o_na, acc_na, out_na)
    _sw_kernel(*sw_in, o_sw, acc_sw, out_sw, off=off, ctx_len=ctx_len, total_len=total_len)
    _mla_kernel(*ml_in, o_ml, acc_ml, out_ml, off=off)


def _local(na_q, na_k, na_v, bias, sw_q, sw_k, sw_v, sink_rows, m_q, m_k, m_v, off, ctx_len):
    B, _, Lc = na_q.shape
    nq = Lc // TM - off
    na_specs, na_ops = _na_operands(na_q, na_k, na_v, bias, off)
    sw_specs, sw_ops = _sw_operands(sw_q, sw_k, sw_v, sink_rows, off, ctx_len)
    ospec = pl.BlockSpec((1, TM, BRANCH_W), lambda b, i: (b, i, 0))
    oshape = jax.ShapeDtypeStruct((B, nq * TM, BRANCH_W), bf16)
    scratch = [pltpu.VMEM((BRANCH_W // HEAD_DIM * V_EXT, TM), f32), pltpu.VMEM((BRANCH_W, TM), f32)]
    return pl.pallas_call(
        functools.partial(_local_kernel, n_na=len(na_ops), n_sw=len(sw_ops), off=off, ctx_len=ctx_len,
                          total_len=Lc),
        grid=(B, nq),
        in_specs=na_specs + sw_specs + [
            pl.BlockSpec((1, MLA_W, TM), lambda b, i: (b, 0, i + off)),
            pl.BlockSpec((1, Lc, MLA_W), lambda b, i: (b, 0, 0)),
            pl.BlockSpec((1, Lc // TM, MLA_HEADS * V_EXT, TM), lambda b, i: (b, 0, 0, 0)),
        ],
        out_specs=[ospec, ospec, ospec],
        out_shape=[oshape, oshape, oshape],
        scratch_shapes=scratch + scratch + scratch,
        compiler_params=_cparams("parallel", "arbitrary"),
        name="attn",
    )(*na_ops, *sw_ops, m_q, m_k, m_v)


def _mla_kernel(q_ref, k_ref, v_ref, o_ref, acc_ref, out_ref, *, off):
    nchunk = v_ref.shape[1]
    tq = q_ref.shape[2]

    def attend(chunks):
        acc_ref[...] = jnp.zeros_like(acc_ref)
        ms = [jnp.full((1, tq), NEG_INF, f32) for _ in range(MLA_HEADS)]

        def scores(u):
            h, c = u % MLA_HEADS, chunks[u // MLA_HEADS]
            pair = slice((h // 2) * 2 * MLA_PAD, (h // 2 + 1) * 2 * MLA_PAD)
            qh = q_ref[0, h * MLA_PAD:(h + 1) * MLA_PAD, :]
            zeros = jnp.zeros_like(qh)
            qpad = jnp.concatenate([qh, zeros] if h % 2 == 0 else [zeros, qh], axis=0)
            return _dot(k_ref[0, c * TM:(c + 1) * TM, pair], qpad)

        def softmax(u, s):
            h = u % MLA_HEADS
            m_old = ms[h]
            ms[h] = jnp.maximum(m_old, jnp.max(s, axis=0, keepdims=True))
            return jnp.exp2(m_old - ms[h]), jnp.exp2(s - ms[h]).astype(bf16)

        def weighted_values(u, item):
            h, c = u % MLA_HEADS, chunks[u // MLA_HEADS]
            alpha, p = item
            rows = slice(h * V_EXT, (h + 1) * V_EXT)
            acc_ref[rows, :] = alpha * acc_ref[rows, :] + _dot(v_ref[0, c, rows, :], p)

        _pipelined_heads(MLA_HEADS * len(chunks), scores, softmax, weighted_values, *MLA_PIPE)
        for h in range(MLA_HEADS):
            num = acc_ref[h * V_EXT:h * V_EXT + MLA_V, :]
            out_ref[h * MLA_V:(h + 1) * MLA_V, :] = num / acc_ref[h * V_EXT + MLA_V:h * V_EXT + MLA_V + 1, :]
        o_ref[0] = out_ref[...].T.astype(bf16)

    if off == 0:
        qt = pl.program_id(1)
        pl.when(qt == 0)(lambda: attend([0]))
        pl.when(qt != 0)(lambda: attend(list(range(nchunk))))
    else:
        attend(list(range(nchunk)))


def _mla(q_t, k, v_c, off):
    B, _, Lc = q_t.shape
    T = Lc // TM
    nq = T - off
    return pl.pallas_call(
        functools.partial(_mla_kernel, off=off),
        grid=(B, nq),
        in_specs=[
            pl.BlockSpec((1, MLA_HEADS * MLA_PAD, TM), lambda b, i: (b, 0, i + off)),
            pl.BlockSpec((1, Lc, MLA_HEADS * MLA_PAD), lambda b, i: (b, 0, 0)),
            pl.BlockSpec((1, T, MLA_HEADS * V_EXT, TM), lambda b, i: (b, 0, 0, 0)),
        ],
        out_specs=pl.BlockSpec((1, TM, BRANCH_W), lambda b, i: (b, i, 0)),
        out_shape=jax.ShapeDtypeStruct((B, nq * TM, BRANCH_W), bf16),
        scratch_shapes=[pltpu.VMEM((MLA_HEADS * V_EXT, TM), f32), pltpu.VMEM((BRANCH_W, TM), f32)],
        compiler_params=_cparams("parallel", "arbitrary"),
        name="mla_attn",
    )(q_t, k, v_c)


def _merge_kernel(c_ref, x_ref, a_ref, s_ref, ga_ref, ona_ref, osw_ref, oml_ref, wg_ref, wb_ref, wo_ref, o_ref,
                  *, off):
    x = _stream_tile(c_ref, x_ref, pl.program_id(1) + off)
    d = x.shape[1]
    hb = _norm_mod(x, a_ref[0], s_ref[0]).astype(bf16)
    z = None
    for n, o_n in enumerate((ona_ref, osw_ref, oml_ref)):
        g = jax.nn.sigmoid(_dot(hb, wg_ref[:, n * d:(n + 1) * d]))
        y = g * _dot(o_n[0], wb_ref[n])
        z = y if z is None else z + y
    o_ref[0] = x + ga_ref[0] * _dot(z.astype(bf16), wo_ref[...])


def _merge(c_src, x_src, a_row, s_row, g_row, o_na, o_sw, o_ml, w_gate, w_branch, w_out, off):
    B, _, D = x_src.shape
    nt = x_src.shape[1] // TM + (0 if c_src is x_src else 1) - off
    nb = a_row.shape[0] - 1
    row = lambda b, i: (jnp.where(i + off == 0, nb, b), 0, 0)
    ospec = pl.BlockSpec((1, TM, BRANCH_W), lambda b, i: (b, i, 0))
    return pl.pallas_call(
        functools.partial(_merge_kernel, off=off),
        grid=(B, nt),
        in_specs=[
            *_stream_specs(c_src, x_src, off),
            pl.BlockSpec((1, 1, D), row), pl.BlockSpec((1, 1, D), row), pl.BlockSpec((1, 1, D), row),
            ospec, ospec, ospec,
            _const_spec(w_gate.shape), _const_spec(w_branch.shape), _const_spec(w_out.shape),
        ],
        out_specs=pl.BlockSpec((1, TM, D), lambda b, i: (b, i, 0)),
        out_shape=jax.ShapeDtypeStruct((B, nt * TM, D), f32),
        compiler_params=_cparams("parallel", "parallel"),
        name="merge",
    )(c_src, x_src, a_row, s_row, g_row, o_na, o_sw, o_ml, w_gate, w_branch, w_out)


def _ffn_kernel(x_ref, xp_ref, xn_ref, a_ref, s_ref, gf_ref, wup_ref, cw_ref, cb_ref, wdn_ref, o_ref, u_ref,
                *, ctx_tiles):
    tm = x_ref.shape[1]
    halo = xp_ref.shape[1]
    nc = wup_ref.shape[0]
    t = pl.program_id(1)
    nt = pl.num_programs(1)
    x = x_ref[0]
    xe = jnp.concatenate([xp_ref[0], x, xn_ref[0]], axis=0)
    hb = _norm_mod(xe, a_ref[0], s_ref[0]).astype(bf16)
    left_ok = jnp.where(jnp.logical_and(t != 0, t != ctx_tiles), 1.0, 0.0)
    right_ok = jnp.where(jnp.logical_and(t != nt - 1, t != ctx_tiles - 1), 1.0, 0.0)

    def up(c):
        u = _dot(hb, wup_ref[c])
        slot = u_ref.at[c % 2]
        slot[:halo] = u[:halo] * left_ok
        slot[halo:halo + tm] = u[halo:halo + tm]
        slot[halo + tm:] = u[halo + tm:] * right_ok

    def conv_act(c):
        slot = u_ref.at[c % 2]
        w = cw_ref[c]
        uc = (w[0:1] * slot[halo - 1:halo - 1 + tm] + w[1:2] * slot[halo:halo + tm]
              + w[2:3] * slot[halo + 1:halo + 1 + tm] + cb_ref[c])
        g, v = uc[:, :FF_CHUNK], uc[:, FF_CHUNK:]
        return (g * jax.nn.sigmoid(g) * v).astype(bf16)

    acc = jnp.zeros(x.shape, f32)
    pending = []
    up(0)
    for c in range(nc):
        if c + 1 < nc:
            up(c + 1)
        pending.append((c, conv_act(c)))
        if len(pending) > FF_DOWN_BEHIND:
            j, act = pending.pop(0)
            acc = acc + _dot(act, wdn_ref[j])
    for j, act in pending:
        acc = acc + _dot(act, wdn_ref[j])
    o_ref[0] = x + gf_ref[0] * acc


def _ffn(x1, a_row, s_row, g_row, w_up_c, conv_w_c, conv_b_c, w_down_c, ctx_tiles):
    B, Lt, D = x1.shape
    nt = Lt // TM
    nb = a_row.shape[0] - 1
    per = TM // SUBLANES
    last = Lt // SUBLANES - 1
    row = lambda b, t: (jnp.where(t < ctx_tiles, nb, b), 0, 0)
    return pl.pallas_call(
        functools.partial(_ffn_kernel, ctx_tiles=ctx_tiles),
        grid=(B, nt),
        in_specs=[
            pl.BlockSpec((1, TM, D), lambda b, t: (b, t, 0)),
            pl.BlockSpec((1, SUBLANES, D), lambda b, t: (b, jnp.maximum(t * per - 1, 0), 0)),
            pl.BlockSpec((1, SUBLANES, D), lambda b, t: (b, jnp.minimum((t + 1) * per, last), 0)),
            pl.BlockSpec((1, 1, D), row), pl.BlockSpec((1, 1, D), row), pl.BlockSpec((1, 1, D), row),
            _const_spec(w_up_c.shape), _const_spec(conv_w_c.shape), _const_spec(conv_b_c.shape),
            _const_spec(w_down_c.shape),
        ],
        out_specs=pl.BlockSpec((1, TM, D), lambda b, t: (b, t, 0)),
        out_shape=jax.ShapeDtypeStruct((B, Lt, D), f32),
        scratch_shapes=[pltpu.VMEM((2, TM + 2 * SUBLANES, 2 * FF_CHUNK), f32)],
        compiler_params=_cparams("parallel", "arbitrary"),
        name="ffn",
    )(x1, x1, x1, a_row, s_row, g_row, w_up_c, conv_w_c, conv_b_c, w_down_c)


def _rope_table(seq, ctx_len, n):
    t = jnp.arange(seq)
    inv = ROPE_BASE ** (-jnp.arange(n, dtype=f32) / n)
    ar = (t // GRID_W).astype(f32)[None, :] * inv[:, None]
    ac = (t % GRID_W).astype(f32)[None, :] * inv[:, None]
    lat = jnp.concatenate([jnp.cos(ar), jnp.sin(ar), jnp.cos(ac), jnp.sin(ac)], axis=0)
    one, zero = jnp.ones((n, ctx_len), f32), jnp.zeros((n, ctx_len), f32)
    return jnp.concatenate([jnp.concatenate([one, zero, one, zero], axis=0), lat], axis=1)


def _chunked_ffn_weights(w_up, conv_w, conv_b, w_down):
    d_ff = w_down.shape[0]
    nc = d_ff // FF_CHUNK
    pair = lambda a: jnp.concatenate([a[..., :d_ff].reshape(a.shape[:-1] + (nc, FF_CHUNK)),
                                      a[..., d_ff:].reshape(a.shape[:-1] + (nc, FF_CHUNK))], axis=-1)
    w_up_c = jnp.moveaxis(pair(w_up), 1, 0).astype(bf16)
    conv_w_c = jnp.moveaxis(pair(conv_w), 1, 0)
    conv_b_c = pair(conv_b)[:, None, :]
    w_down_c = w_down.reshape(nc, FF_CHUNK, w_down.shape[1]).astype(bf16)
    return w_up_c, conv_w_c, conv_b_c, w_down_c


def kernel(x, c, ctx, c_ctx, w_ada, b_ada, g_mix, g_ffn, w_in, na_q_norm, na_k_norm, na_rpb, sw_q_norm, sw_k_norm,
           sw_sink, mla_q_rank_norm, mla_kv_rank_norm, w_uq, w_ukv, mla_q_norm, mla_k_norm, w_branch, w_out,
           w_up, conv_w, conv_b, w_down):
    B, S, D = x.shape
    C = ctx.shape[1]
    depth = w_ada.shape[0]
    assert C == TM and S % TM == 0 and GRID_W * NA_QROWS == TM and (S // GRID_W) * GRID_W == S
    assert S // GRID_W >= NA_KEYS // GRID_W and w_down.shape[1] % FF_CHUNK == 0
    rows = S // GRID_W

    c_src, x_src = ctx, x
    n_mod = -(-(B + 1) // SUBLANES) * SUBLANES
    c_all = jnp.zeros((n_mod, D), f32).at[:B].set(c).at[B].set(c_ctx)
    zero = jnp.zeros_like(g_mix)
    g_sel = jnp.stack([zero, g_mix, zero, zero, g_ffn, zero], axis=1)[:, :, None, :]
    mod = _ada(c_all, w_ada, b_ada, g_sel)

    rope_sw = _rope_table(S, C, HEAD_DIM // 4)
    rope_ml = _rope_table(S, C, MLA_ROPE // 4)
    group = SW_Q_HEADS // SW_KV_HEADS

    for l in range(depth):
        off = 0 if l < depth - 1 else 1
        m6 = mod[l, :B + 1].reshape(B + 1, 6, 1, D)
        shift_a, a_mix, gate_a, shift_f, a_ffn, gate_f = (m6[:, j] for j in range(6))
        w_in_t = w_in[l][:, :N_STREAM].T.astype(bf16)
        gains = jnp.concatenate([na_q_norm[l], na_k_norm[l], sw_q_norm[l], sw_k_norm[l], mla_q_rank_norm[l],
                                 mla_kv_rank_norm[l], mla_q_norm[l], mla_k_norm[l]])[:, None]
        (na_q, na_k, na_v, sw_q, sw_k, sw_v, m_q, m_k, m_v) = _proj(
            c_src, x_src, a_mix, shift_a, w_in_t, w_uq[l].T.astype(bf16), w_ukv[l].T.astype(bf16),
            gains, rope_sw, rope_ml)
        sink_rows = jnp.broadcast_to((sw_sink[l] * LOG2E)[:, None, None], (SW_Q_HEADS, 1, TM))
        o_na, o_sw, o_ml = _local(na_q, na_k, na_v, _na_bias_table(na_rpb[l], rows), sw_q, sw_k, sw_v, sink_rows,
                                  m_q, m_k, m_v, off, C)
        x1 = _merge(c_src, x_src, a_mix, shift_a, gate_a, o_na, o_sw, o_ml, w_in[l][:, N_STREAM:].astype(bf16),
                    w_branch[l].astype(bf16), w_out[l].astype(bf16), off)
        xc = _ffn(x1, a_ffn, shift_f, gate_f, *_chunked_ffn_weights(w_up[l], conv_w[l], conv_b[l], w_down[l]),
                  ctx_tiles=1 - off)
        c_src = x_src = xc
    return xc
```
